```python
import jax, jax.numpy as jnp
from jax import lax
import numpy as np

D_MODEL = 1024
BATCH = 2
SEQ = 8192
DEPTH = 2
DEC_BATCH = 32
DEC_SEQ = 64
PAST_LEN = 2048

CHUNK = 64
D_PLE = 256
D_FF = 2816
WA = 512
CONV_W = 31
CONV_HIST = CONV_W - 1
WB = 512
SGU_GROUPS = 4
SGU_CHUNK = 128
SGU_GC = WB // SGU_GROUPS
WC = 512
POOL_WINDOWS = (2, 4, 8, 16)
POOL_GROUPS = len(POOL_WINDOWS)
POOL_GC = WC // POOL_GROUPS
POOL_HIST = max(POOL_WINDOWS) - 1
N_BRANCH = 3
D_IN = 2 * WA + 2 * WB + WC + N_BRANCH * D_MODEL
EPS = 1e-6

kernel_name = "gated_hybrid_streaming_encoder_step"


def rmsnorm(x, g):
    x32 = x.astype(jnp.float32)
    y = x32 * lax.rsqrt(jnp.mean(x32 * x32, axis=-1, keepdims=True) + EPS)
    return (y * g.astype(jnp.float32)).astype(x.dtype)


def layernorm(x, g, b):
    x32 = x.astype(jnp.float32)
    mu = jnp.mean(x32, axis=-1, keepdims=True)
    xc = x32 - mu
    y = xc * lax.rsqrt(jnp.mean(xc * xc, axis=-1, keepdims=True) + EPS)
    return (y * g.astype(jnp.float32) + b.astype(jnp.float32)).astype(x.dtype)


def swiglu(x, w_gate, w_up, w_down):
    return (jax.nn.silu(x @ w_gate) * (x @ w_up)) @ w_down


def conv_module(a, b, hist, dw_w, dw_b, ln_g, ln_b, w_out):
    z = a * jax.nn.sigmoid(b)
    full = jnp.concatenate([hist.astype(z.dtype), z], axis=1)
    y = lax.conv_general_dilated(full, dw_w[:, None, :].astype(z.dtype), window_strides=(1,),
                                 padding='VALID', dimension_numbers=('NWC', 'WIO', 'NWC'),
                                 feature_group_count=WA) + dw_b
    y = jax.nn.silu(layernorm(y, ln_g, ln_b))
    return y @ w_out, full[:, -CONV_HIST:]


def spatial_gating(u, vn, w_s, b_s):
    B, S, _ = vn.shape
    n_chunks = -(-S // SGU_CHUNK)
    pad = n_chunks * SGU_CHUNK - S
    vp = jnp.pad(vn, ((0, 0), (0, pad), (0, 0))).reshape(B, n_chunks, SGU_CHUNK, SGU_GROUPS, SGU_GC)
    mask = jnp.tril(jnp.ones((SGU_CHUNK, SGU_CHUNK), dtype=bool))
    ws = jnp.where(mask[None], w_s, jnp.zeros_like(w_s))
    s = jnp.einsum('gij,bnjgc->bnigc', ws, vp) + b_s.T[None, None, :, :, None]
    s = s.reshape(B, n_chunks * SGU_CHUNK, WB)[:, :S]
    return u * s


def pool_mixer(c, hist, start, w_lin, scale, w_out):
    B, S, _ = c.shape
    full = jnp.concatenate([hist.astype(c.dtype), c], axis=1)
    cs = jnp.cumsum(full.astype(jnp.float32), axis=1)
    cs = jnp.concatenate([jnp.zeros((B, 1, WC), jnp.float32), cs], axis=1)
    pos = start + jnp.arange(S)
    means = []
    for g, w in enumerate(POOL_WINDOWS):
        cg = cs[..., g * POOL_GC:(g + 1) * POOL_GC]
        hi = cg[:, POOL_HIST + 1:POOL_HIST + 1 + S]
        lo = cg[:, POOL_HIST + 1 - w:POOL_HIST + 1 - w + S]
        cnt = jnp.minimum(pos + 1, w).astype(jnp.float32)[None, :, None]
        means.append((hi - lo) / cnt)
    mean = jnp.concatenate(means, axis=-1)
    d = (mean - c.astype(jnp.float32)).astype(c.dtype).reshape(B, S, POOL_GROUPS, POOL_GC)
    y = jnp.einsum('bsgc,gcd->bsgd', d, w_lin).reshape(B, S, WC) * scale
    return y @ w_out, full[:, -POOL_HIST:]


def layer(x, p, conv_hist, pool_hist, start, prm):
    h = x + 0.5 * swiglu(rmsnorm(x, prm['ffn1_norm']), prm['ffn1_w_gate'], prm['ffn1_w_up'], prm['ffn1_w_down'])
    n = rmsnorm(h, prm['mix_norm'])
    z = n @ prm['w_in']
    a, b, u, v, c, gl = jnp.split(z, [WA, 2 * WA, 2 * WA + WB, 2 * WA + 2 * WB, 2 * WA + 2 * WB + WC], axis=-1)
    gates = jax.nn.sigmoid(gl + prm['gate_b'])
    g_a, g_b, g_c = jnp.split(gates, N_BRANCH, axis=-1)
    y_a, conv_state = conv_module(a, b, conv_hist, prm['conv_dw_w'], prm['conv_dw_b'],
                                  prm['conv_ln_g'], prm['conv_ln_b'], prm['conv_w_out'])
    vn = layernorm(v, prm['sgu_ln_g'], prm['sgu_ln_b'])
    y_b = spatial_gating(u, vn, prm['sgu_w_s'], prm['sgu_b_s']) @ prm['sgu_w_out']
    y_c, pool_state = pool_mixer(c, pool_hist, start, prm['pool_w'], prm['pool_scale'], prm['pool_w_out'])
    h = h + (g_a * y_a + g_b * y_b + g_c * y_c) @ prm['w_out']
    h = h + 0.5 * swiglu(rmsnorm(h, prm['ffn2_norm']), prm['ffn2_w_gate'], prm['ffn2_w_up'], prm['ffn2_w_down'])
    h = h + jax.nn.sigmoid(rmsnorm(h, prm['pe_norm']) @ prm['pe_w_gate']) * (p @ prm['pe_w_proj'])
    S = x.shape[1]
    cur = ((S - 1) // SGU_CHUNK) * SGU_CHUNK
    return h, conv_state, pool_state, vn[:, cur:]


def setup_inputs(seed: int = 0) -> dict:
    key = jax.random.key(seed)
    ks = iter(jax.random.split(key, 48))
    L = DEPTH

    def nrm(shape, scale):
        return jax.random.normal(next(ks), shape, jnp.float32) * scale

    def gain(shape):
        return 1.0 + nrm(shape, 0.02)

    return {
        "x_prompt": nrm((BATCH, SEQ, D_MODEL), 1.0),
        "x_sample": nrm((DEC_BATCH, DEC_SEQ, D_MODEL), 1.0),
        "p_prompt": nrm((DEPTH, BATCH, SEQ, D_PLE), 1.0),
        "p_sample": nrm((DEPTH, DEC_BATCH, DEC_SEQ, D_PLE), 1.0),
        "cache_conv": nrm((DEPTH, DEC_BATCH, CONV_HIST, WA), 0.5),
        "cache_pool": nrm((DEPTH, DEC_BATCH, POOL_HIST, WC), 1.0),
        "ffn1_norm": gain((L, D_MODEL)),
        "ffn1_w_gate": nrm((L, D_MODEL, D_FF), D_MODEL ** -0.5),
        "ffn1_w_up": nrm((L, D_MODEL, D_FF), D_MODEL ** -0.5),
        "ffn1_w_down": nrm((L, D_FF, D_MODEL), D_FF ** -0.5),
        "mix_norm": gain((L, D_MODEL)),
        "w_in": nrm((L, D_MODEL, D_IN), D_MODEL ** -0.5),
        "gate_b": nrm((L, N_BRANCH * D_MODEL), 0.1),
        "conv_dw_w": nrm((L, CONV_W, WA), CONV_W ** -0.5),
        "conv_dw_b": nrm((L, WA), 0.02),
        "conv_ln_g": gain((L, WA)),
        "conv_ln_b": nrm((L, WA), 0.02),
        "conv_w_out": nrm((L, WA, D_MODEL), WA ** -0.5),
        "sgu_ln_g": gain((L, WB)),
        "sgu_ln_b": nrm((L, WB), 0.02),
        "sgu_w_s": nrm((L, SGU_GROUPS, SGU_CHUNK, SGU_CHUNK), SGU_CHUNK ** -0.5),
        "sgu_b_s": gain((L, SGU_GROUPS, SGU_CHUNK)),
        "sgu_w_out": nrm((L, WB, D_MODEL), WB ** -0.5),
        "pool_w": nrm((L, POOL_GROUPS, POOL_GC, POOL_GC), POOL_GC ** -0.5),
        "pool_scale": gain((L, WC)),
        "pool_w_out": nrm((L, WC, D_MODEL), WC ** -0.5),
        "w_out": nrm((L, D_MODEL, D_MODEL), D_MODEL ** -0.5),
        "ffn2_norm": gain((L, D_MODEL)),
        "ffn2_w_gate": nrm((L, D_MODEL, D_FF), D_MODEL ** -0.5),
        "ffn2_w_up": nrm((L, D_MODEL, D_FF), D_MODEL ** -0.5),
        "ffn2_w_down": nrm((L, D_FF, D_MODEL), D_FF ** -0.5),
        "pe_norm": gain((L, D_MODEL)),
        "pe_w_gate": nrm((L, D_MODEL, D_MODEL), D_MODEL ** -0.5),
        "pe_w_proj": nrm((L, D_PLE, D_MODEL), D_PLE ** -0.5),
        "final_norm": gain((D_MODEL,)),
    }


def reference(x_prompt, x_sample, p_prompt, p_sample, cache_conv, cache_pool,
              ffn1_norm, ffn1_w_gate, ffn1_w_up, ffn1_w_down, mix_norm, w_in, gate_b,
              conv_dw_w, conv_dw_b, conv_ln_g, conv_ln_b, conv_w_out,
              sgu_ln_g, sgu_ln_b, sgu_w_s, sgu_b_s, sgu_w_out,
              pool_w, pool_scale, pool_w_out, w_out,
              ffn2_norm, ffn2_w_gate, ffn2_w_up, ffn2_w_down,
              pe_norm, pe_w_gate, pe_w_proj, final_norm):
    bp = x_prompt.shape[0]
    conv_zero = jnp.zeros((bp, CONV_HIST, WA), x_prompt.dtype)
    pool_zero = jnp.zeros((bp, POOL_HIST, WC), x_prompt.dtype)
    hp, hs = x_prompt, x_sample
    conv_p, conv_s, pool_p, pool_s, v_p, v_s = [], [], [], [], [], []
    for i in range(DEPTH):
        prm = {
            'ffn1_norm': ffn1_norm[i], 'ffn1_w_gate': ffn1_w_gate[i], 'ffn1_w_up': ffn1_w_up[i],
            'ffn1_w_down': ffn1_w_down[i], 'mix_norm': mix_norm[i], 'w_in': w_in[i], 'gate_b': gate_b[i],
            'conv_dw_w': conv_dw_w[i], 'conv_dw_b': conv_dw_b[i], 'conv_ln_g': conv_ln_g[i],
            'conv_ln_b': conv_ln_b[i], 'conv_w_out': conv_w_out[i],
            'sgu_ln_g': sgu_ln_g[i], 'sgu_ln_b': sgu_ln_b[i], 'sgu_w_s': sgu_w_s[i], 'sgu_b_s': sgu_b_s[i],
            'sgu_w_out': sgu_w_out[i], 'pool_w': pool_w[i], 'pool_scale': pool_scale[i],
            'pool_w_out': pool_w_out[i], 'w_out': w_out[i],
            'ffn2_norm': ffn2_norm[i], 'ffn2_w_gate': ffn2_w_gate[i], 'ffn2_w_up': ffn2_w_up[i],
            'ffn2_w_down': ffn2_w_down[i], 'pe_norm': pe_norm[i], 'pe_w_gate': pe_w_gate[i],
            'pe_w_proj': pe_w_proj[i],
        }
        hp, cp, pp, vp = layer(hp, p_prompt[i], conv_zero, pool_zero, 0, prm)
        hs, cs, ps, vs = layer(hs, p_sample[i], cache_conv[i], cache_pool[i], PAST_LEN, prm)
        conv_p.append(cp); conv_s.append(cs); pool_p.append(pp); pool_s.append(ps)
        v_p.append(vp); v_s.append(vs)
    y_prompt = rmsnorm(hp, final_norm)
    y_sample = rmsnorm(hs, final_norm)
    return (y_prompt, y_sample, jnp.stack(conv_p), jnp.stack(conv_s), jnp.stack(pool_p),
            jnp.stack(pool_s), jnp.stack(v_p), jnp.stack(v_s))
```

```python
import functools

import jax
import jax.numpy as jnp
from jax import lax
from jax.experimental import pallas as pl
from jax.experimental.pallas import tpu as pltpu

D_MODEL = 1024
D_PLE = 256
D_FF = 2816
WA = 512
CONV_W = 31
CONV_HIST = CONV_W - 1
WB = 512
SGU_GROUPS = 4
SGU_CHUNK = 128
SGU_GC = WB // SGU_GROUPS
WC = 512
POOL_WINDOWS = (2, 4, 8, 16)
POOL_GC = WC // len(POOL_WINDOWS)
POOL_HIST = max(POOL_WINDOWS) - 1
PAST_LEN = 2048
EPS = 1e-6

F32 = jnp.float32
BF16 = jnp.bfloat16

V7X_VMEM_LIMIT_BYTES = 60 * 1024 * 1024
SUBLANES = 8
CONV_PAD = 32
POOL_PAD = 16
ROW_BLOCK = 32
FF_CHUNKS = ((0, 1024), (1024, 1024), (2048, 768))


def _rms(x, g):
    return x * lax.rsqrt(jnp.mean(x * x, axis=-1, keepdims=True) + EPS) * g


def _ln(x, g, b):
    mu = jnp.mean(x, axis=-1, keepdims=True)
    xc = x - mu
    return xc * lax.rsqrt(jnp.mean(xc * xc, axis=-1, keepdims=True) + EPS) * g + b


def _dot(a, w):
    return jnp.dot(a.astype(BF16), w, preferred_element_type=F32)


def _half_swiglu(x, norm_ref, wg_ref, wu_ref, wd_ref):
    n = _rms(x, norm_ref[...]).astype(BF16)
    acc = None
    for c0, cw in FF_CHUNKS:
        g = jnp.dot(n, wg_ref[:, c0:c0 + cw], preferred_element_type=F32)
        u = jnp.dot(n, wu_ref[:, c0:c0 + cw], preferred_element_type=F32)
        hid = (g * jax.nn.sigmoid(g) * u).astype(BF16)
        d = jnp.dot(hid, wd_ref[c0:c0 + cw, :], preferred_element_type=F32)
        acc = d if acc is None else acc + d
    return x + 0.5 * acc


def _mixer_kernel(x_ref, hconv_ref, hpool_ref,
                  f_norm, f_wg, f_wu, f_wd,
                  mix_norm, w_in, gate_b,
                  dw_w, dw_b, cln_g, cln_b, conv_wo,
                  sln_g, sln_b, sgu_ws, sgu_bst, sgu_wo,
                  pool_w, pool_scale, pool_wo, w_out,
                  h_out, conv_out, pool_out, vn_out,
                  cbuf, pbuf, *, nseg, seg_len, carry, start):
    T = nseg * seg_len
    L = seg_len
    tile = pl.program_id(1)

    h = _half_swiglu(x_ref[...], f_norm, f_wg, f_wu, f_wd)
    n = _rms(h, mix_norm[...]).astype(BF16)

    def zin(c0, cw):
        return jnp.dot(n, w_in[:, c0:c0 + cw], preferred_element_type=F32)

    if carry:
        @pl.when(tile == 0)
        def _():
            cbuf[0, CONV_PAD - CONV_HIST:CONV_PAD, :] = hconv_ref[0]
            pbuf[0, POOL_PAD - POOL_HIST:POOL_PAD, :] = hpool_ref[0]
    else:
        for s in range(nseg):
            cbuf[s, CONV_PAD - CONV_HIST:CONV_PAD, :] = hconv_ref[s]
            pbuf[s, POOL_PAD - POOL_HIST:POOL_PAD, :] = hpool_ref[s]

    glu = zin(0, WA) * jax.nn.sigmoid(zin(WA, WA))
    for s in range(nseg):
        cbuf[s, CONV_PAD:CONV_PAD + L, :] = glu[s * L:(s + 1) * L]
    taps = [dw_w[k:k + 1, :] for k in range(CONV_W)]
    bias = dw_b[...]
    conv_rows = []
    for s in range(nseg):
        for r0 in range(0, L, ROW_BLOCK):
            acc = None
            for k in range(CONV_W):
                off = CONV_PAD - CONV_HIST + k + r0
                term = taps[k] * cbuf[s, off:off + ROW_BLOCK, :]
                acc = term if acc is None else acc + term
            conv_rows.append(acc + bias)
        conv_out[s] = cbuf[s, L + CONV_PAD - CONV_HIST:L + CONV_PAD, :]
    if carry:
        cbuf[0, 0:CONV_PAD, :] = cbuf[0, L:L + CONV_PAD, :]
    ya = jnp.concatenate(conv_rows, axis=0)
    ya = _ln(ya, cln_g[...], cln_b[...])
    ya = _dot(ya * jax.nn.sigmoid(ya), conv_wo[...])

    u = zin(2 * WA, WB)
    vn = _ln(zin(2 * WA + WB, WB), sln_g[...], sln_b[...])
    if carry:
        vn_out[...] = vn[T - SGU_CHUNK:T]
    else:
        for s in range(nseg):
            vn_out[s] = vn[s * L:(s + 1) * L]
    rows = min(L, SGU_CHUNK)
    ri = lax.broadcasted_iota(jnp.int32, (rows, rows), 0)
    ci = lax.broadcasted_iota(jnp.int32, (rows, rows), 1)
    ws_tril = [jnp.where(ci <= ri, sgu_ws[g, 0:rows, 0:rows], 0.0).astype(BF16) for g in range(SGU_GROUPS)]
    bst = sgu_bst[...]
    vn_b = vn.astype(BF16)
    s_rows = []
    for r0 in range(0, T, rows):
        parts = []
        for g in range(SGU_GROUPS):
            sg = jnp.dot(ws_tril[g], vn_b[r0:r0 + rows, g * SGU_GC:(g + 1) * SGU_GC],
                         preferred_element_type=F32)
            parts.append(sg + bst[0:rows, g:g + 1])
        s_rows.append(jnp.concatenate(parts, axis=1))
    yb = _dot(u * jnp.concatenate(s_rows, axis=0), sgu_wo[...])

    c = zin(2 * WA + 2 * WB, WC)
    for s in range(nseg):
        pbuf[s, POOL_PAD:POOL_PAD + L, :] = c[s * L:(s + 1) * L]
    pos0 = start + (tile * L if carry else 0)
    pos = pos0 + lax.broadcasted_iota(jnp.int32, (L, 1), 0)
    d_rows = []
    for s in range(nseg):
        parts = []
        for g, w in enumerate(POOL_WINDOWS):
            lanes = slice(g * POOL_GC, (g + 1) * POOL_GC)
            tot = None
            for j in range(w):
                term = pbuf[s, POOL_PAD - j:POOL_PAD - j + L, lanes]
                tot = term if tot is None else tot + term
            cnt = jnp.minimum(pos + 1, w).astype(F32)
            parts.append(tot / cnt - pbuf[s, POOL_PAD:POOL_PAD + L, lanes])
        d_rows.append(jnp.concatenate(parts, axis=1))
        pool_out[s] = pbuf[s, L + POOL_PAD - POOL_HIST:L + POOL_PAD, :]
    if carry:
        pbuf[0, 0:POOL_PAD, :] = pbuf[0, L:L + POOL_PAD, :]
    dd = jnp.concatenate(d_rows, axis=0).astype(BF16)
    yc = jnp.concatenate(
        [jnp.dot(dd[:, g * POOL_GC:(g + 1) * POOL_GC], pool_w[g], preferred_element_type=F32)
         for g in range(len(POOL_WINDOWS))], axis=1)
    yc = _dot(yc * pool_scale[...], pool_wo[...])

    gb = gate_b[...]
    g0 = 2 * WA + 2 * WB + WC
    m = jax.nn.sigmoid(zin(g0, D_MODEL) + gb[:, 0:D_MODEL]) * ya
    m = m + jax.nn.sigmoid(zin(g0 + D_MODEL, D_MODEL) + gb[:, D_MODEL:2 * D_MODEL]) * yb
    m = m + jax.nn.sigmoid(zin(g0 + 2 * D_MODEL, D_MODEL) + gb[:, 2 * D_MODEL:3 * D_MODEL]) * yc
    h_out[...] = h + _dot(m, w_out[...])


def _ffn_kernel(h_ref, p_ref, f_norm, f_wg, f_wu, f_wd, pe_norm, pe_wg, pe_wp, fin_norm, o_ref, *, final):
    h = _half_swiglu(h_ref[...], f_norm, f_wg, f_wu, f_wd)
    gate = jax.nn.sigmoid(_dot(_rms(h, pe_norm[...]), pe_wg[...]))
    h = h + gate * _dot(p_ref[...], pe_wp[...])
    if final:
        h = _rms(h, fin_norm[...])
    o_ref[...] = h


def _resident(arr):
    del arr
    return pl.BlockSpec(memory_space=pltpu.VMEM)


def _mixer_call(x, hconv, hpool, wts, *, nseg, seg_len, carry, start):
    B, S, _ = x.shape
    T = nseg * seg_len
    n_tiles = S // T
    n_streams = hconv.shape[0]
    if carry:
        state_idx = lambda b, i: (b, 0, 0)
        vn_rows = SGU_CHUNK
    else:
        state_idx = lambda b, i: (i, 0, 0)
        vn_rows = seg_len
    in_specs = [
        pl.BlockSpec((None, T, D_MODEL), lambda b, i: (b, i, 0)),
        pl.BlockSpec((nseg, CONV_HIST, WA), state_idx),
        pl.BlockSpec((nseg, POOL_HIST, WC), state_idx),
    ] + [_resident(w) for w in wts]
    out_shape = (
        jax.ShapeDtypeStruct((B, S, D_MODEL), F32),
        jax.ShapeDtypeStruct((n_streams, CONV_HIST, WA), F32),
        jax.ShapeDtypeStruct((n_streams, POOL_HIST, WC), F32),
        jax.ShapeDtypeStruct((n_streams, vn_rows, WB), F32),
    )
    out_specs = (
        pl.BlockSpec((None, T, D_MODEL), lambda b, i: (b, i, 0)),
        pl.BlockSpec((nseg, CONV_HIST, WA), state_idx),
        pl.BlockSpec((nseg, POOL_HIST, WC), state_idx),
        pl.BlockSpec((None, vn_rows, WB) if carry else (nseg, vn_rows, WB), state_idx),
    )
    return pl.pallas_call(
        functools.partial(_mixer_kernel, nseg=nseg, seg_len=seg_len, carry=carry, start=start),
        grid=(B, n_tiles),
        in_specs=in_specs,
        out_specs=out_specs,
        out_shape=out_shape,
        scratch_shapes=[
            pltpu.VMEM((nseg, CONV_PAD + seg_len, WA), F32),
            pltpu.VMEM((nseg, POOL_PAD + seg_len, WC), F32),
        ],
        compiler_params=pltpu.CompilerParams(
            dimension_semantics=("arbitrary", "arbitrary"),
            vmem_limit_bytes=V7X_VMEM_LIMIT_BYTES),
        name="mixer_carry" if carry else "mixer_streams",
    )(x, hconv, hpool, *wts)


def _ffn_call(h, p, wts, *, tile, final):
    N = h.shape[0]
    in_specs = [
        pl.BlockSpec((tile, D_MODEL), lambda i: (i, 0)),
        pl.BlockSpec((tile, D_PLE), lambda i: (i, 0)),
    ] + [_resident(w) for w in wts]
    return pl.pallas_call(
        functools.partial(_ffn_kernel, final=final),
        grid=(N // tile,),
        in_specs=in_specs,
        out_specs=pl.BlockSpec((tile, D_MODEL), lambda i: (i, 0)),
        out_shape=jax.ShapeDtypeStruct((N, D_MODEL), F32),
        compiler_params=pltpu.CompilerParams(
            dimension_semantics=("arbitrary",),
            vmem_limit_bytes=V7X_VMEM_LIMIT_BYTES),
        name="ffn_pe",
    )(h, p, *wts)


PROMPT_TILE = 256
SAMPLE_STREAMS_PER_TILE = 4
FFN_TILE = 256


def kernel(x_prompt, x_sample, p_prompt, p_sample, cache_conv, cache_pool, ffn1_norm, ffn1_w_gate, ffn1_w_up, ffn1_w_down, mix_norm, w_in, gate_b, conv_dw_w, conv_dw_b, conv_ln_g, conv_ln_b, conv_w_out, sgu_ln_g, sgu_ln_b, sgu_w_s, sgu_b_s, sgu_w_out, pool_w, pool_scale, pool_w_out, w_out, ffn2_norm, ffn2_w_gate, ffn2_w_up, ffn2_w_down, pe_norm, pe_w_gate, pe_w_proj, final_norm):
    depth = ffn1_norm.shape[0]
    bp, sp, _ = x_prompt.shape
    bs, ss, _ = x_sample.shape
    row = lambda v: v.reshape(1, -1)
    bf = lambda v: v.astype(BF16)

    hp = x_prompt
    hs = x_sample.reshape(1, bs * ss, D_MODEL)
    conv_zero = jnp.zeros((bp, CONV_HIST, WA), F32)
    pool_zero = jnp.zeros((bp, POOL_HIST, WC), F32)
    fin = row(final_norm)
    conv_p, conv_s, pool_p, pool_s, v_p, v_s = [], [], [], [], [], []
    for i in range(depth):
        mixer_w = (
            row(ffn1_norm[i]), bf(ffn1_w_gate[i]), bf(ffn1_w_up[i]), bf(ffn1_w_down[i]),
            row(mix_norm[i]), bf(w_in[i]), row(gate_b[i]),
            conv_dw_w[i], row(conv_dw_b[i]), row(conv_ln_g[i]), row(conv_ln_b[i]), bf(conv_w_out[i]),
            row(sgu_ln_g[i]), row(sgu_ln_b[i]), sgu_w_s[i], sgu_b_s[i].T, bf(sgu_w_out[i]),
            bf(pool_w[i]), row(pool_scale[i]), bf(pool_w_out[i]), bf(w_out[i]),
        )
        ffn_w = (
            row(ffn2_norm[i]), bf(ffn2_w_gate[i]), bf(ffn2_w_up[i]), bf(ffn2_w_down[i]),
            row(pe_norm[i]), bf(pe_w_gate[i]), bf(pe_w_proj[i]), fin,
        )
        final = i == depth - 1

        hp, cp, pp, vp = _mixer_call(hp, conv_zero, pool_zero, mixer_w,
                                     nseg=1, seg_len=PROMPT_TILE, carry=True, start=0)
        hp = _ffn_call(hp.reshape(bp * sp, D_MODEL), p_prompt[i].reshape(bp * sp, D_PLE), ffn_w,
                       tile=FFN_TILE, final=final).reshape(bp, sp, D_MODEL)

        hs, cs, ps, vs = _mixer_call(hs, cache_conv[i], cache_pool[i], mixer_w,
                                     nseg=SAMPLE_STREAMS_PER_TILE, seg_len=ss, carry=False, start=PAST_LEN)
        hs = _ffn_call(hs.reshape(bs * ss, D_MODEL), p_sample[i].reshape(bs * ss, D_PLE), ffn_w,
                       tile=FFN_TILE, final=final).reshape(1, bs * ss, D_MODEL)

        conv_p.append(cp); conv_s.append(cs); pool_p.append(pp); pool_s.append(ps)
        v_p.append(vp); v_s.append(vs)
    return (hp, hs.reshape(bs, ss, D_MODEL), jnp.stack(conv_p), jnp.stack(conv_s), jnp.stack(pool_p),
            jnp.stack(pool_s), jnp.stack(v_p), jnp.stack(v_s))
```

```python
import functools

import jax
import jax.numpy as jnp
from jax import lax
from jax.experimental import pallas as pl
from jax.experimental.pallas import tpu as pltpu

D_MODEL = 1024
D_PLE = 256
D_FF = 2816
WA = 512
CONV_W = 31
CONV_HIST = CONV_W - 1
WB = 512
SGU_GROUPS = 4
SGU_CHUNK = 128
SGU_GC = WB // SGU_GROUPS
WC = 512
POOL_WINDOWS = (2, 4, 8, 16)
POOL_GC = WC // len(POOL_WINDOWS)
POOL_HIST = max(POOL_WINDOWS) - 1
PAST_LEN = 2048
EPS = 1e-6

F32 = jnp.float32
BF16 = jnp.bfloat16

V7X_VMEM_LIMIT_BYTES = 60 * 1024 * 1024
LANES = 128
CONV_PAD = 32
POOL_PAD = 16
ROW_BLOCK = 32
GATE_COLS = 512
FF_CHUNKS = ((0, 1024), (1024, 1024), (2048, 768))


def _rms(x, g):
    return x * lax.rsqrt(jnp.mean(x * x, axis=-1, keepdims=True) + EPS) * g


def _ln(x, g, b):
    mu = jnp.mean(x, axis=-1, keepdims=True)
    xc = x - mu
    return xc * lax.rsqrt(jnp.mean(xc * xc, axis=-1, keepdims=True) + EPS) * g + b


def _dot(a, w):
    return jnp.dot(a.astype(BF16), w, preferred_element_type=F32)


def _half_swiglu(x, norm_ref, wg_ref, wu_ref, wd_ref):
    n = _rms(x, norm_ref[...]).astype(BF16)
    acc = None
    for c0, cw in FF_CHUNKS:
        g = jnp.dot(n, wg_ref[:, c0:c0 + cw], preferred_element_type=F32)
        u = jnp.dot(n, wu_ref[:, c0:c0 + cw], preferred_element_type=F32)
        hid = (g * jax.nn.sigmoid(g) * u).astype(BF16)
        d = jnp.dot(hid, wd_ref[c0:c0 + cw, :], preferred_element_type=F32)
        acc = d if acc is None else acc + d
    return x + 0.5 * acc


def _mixer_kernel(x_ref, hconv_ref, hpool_ref,
                  f_norm, f_wg, f_wu, f_wd,
                  mix_norm, w_in, gate_b,
                  dw_w, dw_b, cln_g, cln_b, conv_wo,
                  sln_g, sln_b, sgu_ws, sgu_bst, sgu_wo,
                  pool_w, pool_scale, pool_wo, w_out,
                  h_out, conv_out, pool_out, vn_out,
                  cbuf, pbuf, *, nseg, seg_len, carry, start):
    T = nseg * seg_len
    L = seg_len
    tile = pl.program_id(1)
    lane_groups = [slice(g * LANES, (g + 1) * LANES) for g in range(WA // LANES)]

    h = _half_swiglu(x_ref[...], f_norm, f_wg, f_wu, f_wd)
    n = _rms(h, mix_norm[...]).astype(BF16)

    def zin(c0, cw):
        return jnp.dot(n, w_in[:, c0:c0 + cw], preferred_element_type=F32)

    def load_history(s):
        for g, lanes in enumerate(lane_groups):
            cbuf[s, g, CONV_PAD - CONV_HIST:CONV_PAD, :] = hconv_ref[s, :, lanes]
            pbuf[s, g, POOL_PAD - POOL_HIST:POOL_PAD, :] = hpool_ref[s, :, lanes]

    if carry:
        pl.when(tile == 0)(functools.partial(load_history, 0))
    else:
        for s in range(nseg):
            load_history(s)

    glu = zin(0, WA) * jax.nn.sigmoid(zin(WA, WA))
    for s in range(nseg):
        for g, lanes in enumerate(lane_groups):
            cbuf[s, g, CONV_PAD:CONV_PAD + L, :] = glu[s * L:(s + 1) * L, lanes]

    def conv_block(s, r0):
        parts = []
        for g, lanes in enumerate(lane_groups):
            acc = None
            for k in range(CONV_W):
                off = CONV_PAD - CONV_HIST + k + r0
                term = dw_w[k:k + 1, lanes] * cbuf[s, g, off:off + ROW_BLOCK, :]
                acc = term if acc is None else acc + term
            parts.append(acc + dw_b[:, lanes])
        return jnp.concatenate(parts, axis=1)

    c_off = 2 * WA + 2 * WB
    g_off = c_off + WC
    gb = gate_b[...]

    def gate_part(j):
        cols = slice(j * GATE_COLS, (j + 1) * GATE_COLS)
        return jax.nn.sigmoid(zin(g_off + j * GATE_COLS, GATE_COLS) + gb[:, cols])

    mxu_tasks = ([functools.partial(zin, c_off, WC), functools.partial(zin, 2 * WA + WB, WB),
                  functools.partial(zin, 2 * WA, WB)]
                 + [functools.partial(gate_part, j) for j in range(3 * D_MODEL // GATE_COLS)])
    blocks = [(s, r0) for s in range(nseg) for r0 in range(0, L, ROW_BLOCK)]
    conv_rows, done = [], []
    for idx, (s, r0) in enumerate(blocks):
        conv_rows.append(conv_block(s, r0))
        while len(done) < (idx + 1) * len(mxu_tasks) // len(blocks):
            done.append(mxu_tasks[len(done)]())
    c, v, u = done[0], done[1], done[2]
    parts_per_gate = D_MODEL // GATE_COLS
    gates = [jnp.concatenate(done[3 + b * parts_per_gate:3 + (b + 1) * parts_per_gate], axis=1)
             for b in range(3)]

    for s in range(nseg):
        for g, lanes in enumerate(lane_groups):
            conv_out[s, :, lanes] = cbuf[s, g, L + CONV_PAD - CONV_HIST:L + CONV_PAD, :]
    if carry:
        for g in range(len(lane_groups)):
            cbuf[0, g, 0:CONV_PAD, :] = cbuf[0, g, L:L + CONV_PAD, :]

    ya = _ln(jnp.concatenate(conv_rows, axis=0), cln_g[...], cln_b[...])
    m = gates[0] * _dot(ya * jax.nn.sigmoid(ya), conv_wo[...])

    vn = _ln(v, sln_g[...], sln_b[...])
    if carry:
        vn_out[...] = vn[T - SGU_CHUNK:T]
    else:
        for s in range(nseg):
            vn_out[s] = vn[s * L:(s + 1) * L]
    rows = min(L, SGU_CHUNK)
    ri = lax.broadcasted_iota(jnp.int32, (rows, rows), 0)
    ci = lax.broadcasted_iota(jnp.int32, (rows, rows), 1)
    ws_tril = [jnp.where(ci <= ri, sgu_ws[g, 0:rows, 0:rows], 0.0).astype(BF16) for g in range(SGU_GROUPS)]
    bst = sgu_bst[...]
    vn_b = vn.astype(BF16)
    s_rows = []
    for r0 in range(0, T, rows):
        parts = []
        for g in range(SGU_GROUPS):
            sg = jnp.dot(ws_tril[g], vn_b[r0:r0 + rows, g * SGU_GC:(g + 1) * SGU_GC],
                         preferred_element_type=F32)
            parts.append(sg + bst[0:rows, g:g + 1])
        s_rows.append(jnp.concatenate(parts, axis=1))
    m = m + gates[1] * _dot(u * jnp.concatenate(s_rows, axis=0), sgu_wo[...])

    for s in range(nseg):
        for g, lanes in enumerate(lane_groups):
            pbuf[s, g, POOL_PAD:POOL_PAD + L, :] = c[s * L:(s + 1) * L, lanes]
    pos0 = start + (tile * L if carry else 0)
    pos = pos0 + lax.broadcasted_iota(jnp.int32, (L, 1), 0)
    inv_cnt = [1.0 / jnp.minimum(pos + 1, w).astype(F32) for w in POOL_WINDOWS]
    d_rows = []
    for s in range(nseg):
        for r0 in range(0, L, ROW_BLOCK):
            parts = []
            for g, w in enumerate(POOL_WINDOWS):
                tot = None
                for j in range(w):
                    lo = POOL_PAD - j + r0
                    term = pbuf[s, g, lo:lo + ROW_BLOCK, :]
                    tot = term if tot is None else tot + term
                cur = pbuf[s, g, POOL_PAD + r0:POOL_PAD + r0 + ROW_BLOCK, :]
                parts.append(tot * inv_cnt[g][r0:r0 + ROW_BLOCK] - cur)
            d_rows.append(jnp.concatenate(parts, axis=1))
        for g, lanes in enumerate(lane_groups):
            pool_out[s, :, lanes] = pbuf[s, g, L + POOL_PAD - POOL_HIST:L + POOL_PAD, :]
    if carry:
        for g in range(len(lane_groups)):
            pbuf[0, g, 0:POOL_PAD, :] = pbuf[0, g, L:L + POOL_PAD, :]
    dd = jnp.concatenate(d_rows, axis=0).astype(BF16)
    yc = jnp.concatenate(
        [jnp.dot(dd[:, g * POOL_GC:(g + 1) * POOL_GC], pool_w[g], preferred_element_type=F32)
         for g in range(len(POOL_WINDOWS))], axis=1)
    m = m + gates[2] * _dot(yc * pool_scale[...], pool_wo[...])

    h_out[...] = h + _dot(m, w_out[...])


def _ffn_kernel(h_ref, p_ref, f_norm, f_wg, f_wu, f_wd, pe_norm, pe_wg, pe_wp, fin_norm, o_ref, *, final):
    h = _half_swiglu(h_ref[...], f_norm, f_wg, f_wu, f_wd)
    gate = jax.nn.sigmoid(_dot(_rms(h, pe_norm[...]), pe_wg[...]))
    h = h + gate * _dot(p_ref[...], pe_wp[...])
    if final:
        h = _rms(h, fin_norm[...])
    o_ref[...] = h


def _resident(arr):
    del arr
    return pl.BlockSpec(memory_space=pltpu.VMEM)


def _mixer_call(x, hconv, hpool, wts, *, nseg, seg_len, carry, start):
    B, S, _ = x.shape
    T = nseg * seg_len
    n_tiles = S // T
    n_streams = hconv.shape[0]
    if carry:
        state_idx = lambda b, i: (b, 0, 0)
        vn_rows = SGU_CHUNK
    else:
        state_idx = lambda b, i: (i, 0, 0)
        vn_rows = seg_len
    in_specs = [
        pl.BlockSpec((None, T, D_MODEL), lambda b, i: (b, i, 0)),
        pl.BlockSpec((nseg, CONV_HIST, WA), state_idx),
        pl.BlockSpec((nseg, POOL_HIST, WC), state_idx),
    ] + [_resident(w) for w in wts]
    out_shape = (
        jax.ShapeDtypeStruct((B, S, D_MODEL), F32),
        jax.ShapeDtypeStruct((n_streams, CONV_HIST, WA), F32),
        jax.ShapeDtypeStruct((n_streams, POOL_HIST, WC), F32),
        jax.ShapeDtypeStruct((n_streams, vn_rows, WB), F32),
    )
    out_specs = (
        pl.BlockSpec((None, T, D_MODEL), lambda b, i: (b, i, 0)),
        pl.BlockSpec((nseg, CONV_HIST, WA), state_idx),
        pl.BlockSpec((nseg, POOL_HIST, WC), state_idx),
        pl.BlockSpec((None, vn_rows, WB) if carry else (nseg, vn_rows, WB), state_idx),
    )
    return pl.pallas_call(
        functools.partial(_mixer_kernel, nseg=nseg, seg_len=seg_len, carry=carry, start=start),
        grid=(B, n_tiles),
        in_specs=in_specs,
        out_specs=out_specs,
        out_shape=out_shape,
        scratch_shapes=[
            pltpu.VMEM((nseg, WA // LANES, CONV_PAD + seg_len, LANES), F32),
            pltpu.VMEM((nseg, WC // LANES, POOL_PAD + seg_len, LANES), F32),
        ],
        compiler_params=pltpu.CompilerParams(
            dimension_semantics=("arbitrary", "arbitrary"),
            vmem_limit_bytes=V7X_VMEM_LIMIT_BYTES),
        name="mixer_carry" if carry else "mixer_streams",
    )(x, hconv, hpool, *wts)


def _ffn_call(h, p, wts, *, tile, final):
    N = h.shape[0]
    in_specs = [
        pl.BlockSpec((tile, D_MODEL), lambda i: (i, 0)),
        pl.BlockSpec((tile, D_PLE), lambda i: (i, 0)),
    ] + [_resident(w) for w in wts]
    return pl.pallas_call(
        functools.partial(_ffn_kernel, final=final),
        grid=(N // tile,),
        in_specs=in_specs,
        out_specs=pl.BlockSpec((tile, D_MODEL), lambda i: (i, 0)),
        out_shape=jax.ShapeDtypeStruct((N, D_MODEL), F32),
        compiler_params=pltpu.CompilerParams(
            dimension_semantics=("arbitrary",),
            vmem_limit_bytes=V7X_VMEM_LIMIT_BYTES),
        name="ffn_pe",
    )(h, p, *wts)


PROMPT_TILE = 256
SAMPLE_STREAMS_PER_TILE = 4
FFN_TILE = 256


def kernel(x_prompt, x_sample, p_prompt, p_sample, cache_conv, cache_pool, ffn1_norm, ffn1_w_gate, ffn1_w_up, ffn1_w_down, mix_norm, w_in, gate_b, conv_dw_w, conv_dw_b, conv_ln_g, conv_ln_b, conv_w_out, sgu_ln_g, sgu_ln_b, sgu_w_s, sgu_b_s, sgu_w_out, pool_w, pool_scale, pool_w_out, w_out, ffn2_norm, ffn2_w_gate, ffn2_w_up, ffn2_w_down, pe_norm, pe_w_gate, pe_w_proj, final_norm):
    depth = ffn1_norm.shape[0]
    bp, sp, _ = x_prompt.shape
    bs, ss, _ = x_sample.shape
    row = lambda v: v.reshape(1, -1)
    bf = lambda v: v.astype(BF16)

    hp = x_prompt
    hs = x_sample.reshape(1, bs * ss, D_MODEL)
    conv_zero = jnp.zeros((bp, CONV_HIST, WA), F32)
    pool_zero = jnp.zeros((bp, POOL_HIST, WC), F32)
    fin = row(final_norm)
    conv_p, conv_s, pool_p, pool_s, v_p, v_s = [], [], [], [], [], []
    for i in range(depth):
        mixer_w = (
            row(ffn1_norm[i]), bf(ffn1_w_gate[i]), bf(ffn1_w_up[i]), bf(ffn1_w_down[i]),
            row(mix_norm[i]), bf(w_in[i]), row(gate_b[i]),
            conv_dw_w[i], row(conv_dw_b[i]), row(conv_ln_g[i]), row(conv_ln_b[i]), bf(conv_w_out[i]),
            row(sgu_ln_g[i]), row(sgu_ln_b[i]), sgu_w_s[i], sgu_b_s[i].T, bf(sgu_w_out[i]),
            bf(pool_w[i]), row(pool_scale[i]), bf(pool_w_out[i]), bf(w_out[i]),
        )
        ffn_w = (
            row(ffn2_norm[i]), bf(ffn2_w_gate[i]), bf(ffn2_w_up[i]), bf(ffn2_w_down[i]),
            row(pe_norm[i]), bf(pe_w_gate[i]), bf(pe_w_proj[i]), fin,
        )
        final = i == depth - 1

        hp, cp, pp, vp = _mixer_call(hp, conv_zero, pool_zero, mixer_w,
                                     nseg=1, seg_len=PROMPT_TILE, carry=True, start=0)
        hp = _ffn_call(hp.reshape(bp * sp, D_MODEL), p_prompt[i].reshape(bp * sp, D_PLE), ffn_w,
                       tile=FFN_TILE, final=final).reshape(bp, sp, D_MODEL)

        hs, cs, ps, vs = _mixer_call(hs, cache_conv[i], cache_pool[i], mixer_w,
                                     nseg=SAMPLE_STREAMS_PER_TILE, seg_len=ss, carry=False, start=PAST_LEN)
        hs = _ffn_call(hs.reshape(bs * ss, D_MODEL), p_sample[i].reshape(bs * ss, D_PLE), ffn_w,
                       tile=FFN_TILE, final=final).reshape(1, bs * ss, D_MODEL)

        conv_p.append(cp); conv_s.append(cs); pool_p.append(pp); pool_s.append(ps)
        v_p.append(vp); v_s.append(vs)
    return (hp, hs.reshape(bs, ss, D_MODEL), jnp.stack(conv_p), jnp.stack(conv_s), jnp.stack(pool_p),
            jnp.stack(pool_s), jnp.stack(v_p), jnp.stack(v_s))
```

```python
import functools

import jax
import jax.numpy as jnp
from jax import lax
from jax.experimental import pallas as pl
from jax.experimental.pallas import tpu as pltpu

D_MODEL = 1024
D_PLE = 256
D_FF = 2816
WA = 512
CONV_W = 31
CONV_HIST = CONV_W - 1
WB = 512
SGU_GROUPS = 4
SGU_CHUNK = 128
SGU_GC = WB // SGU_GROUPS
WC = 512
POOL_WINDOWS = (2, 4, 8, 16)
POOL_GC = WC // len(POOL_WINDOWS)
POOL_HIST = max(POOL_WINDOWS) - 1
D_IN = 2 * WA + 2 * WB + WC + 3 * D_MODEL
PAST_LEN = 2048
EPS = 1e-6

F32 = jnp.float32
BF16 = jnp.bfloat16

V7X_VMEM_LIMIT_BYTES = 60 * 1024 * 1024
LANES = 128
CONV_PAD = 32
POOL_PAD = 16
ROW_BLOCK = 32
GATE_COLS = 512
FF_CHUNKS = ((0, 1024), (1024, 1024), (2048, 768))
WEIGHT_STEPS = 8
N_MIXER_MATS = 5
N_MIXER_VECS = 12

PROMPT_TILE = 256
SAMPLE_STREAMS_PER_TILE = 4
FFN_TILE = 256


def _rms(x, g):
    return x * lax.rsqrt(jnp.mean(x * x, axis=-1, keepdims=True) + EPS) * g


def _ln(x, g, b):
    mu = jnp.mean(x, axis=-1, keepdims=True)
    xc = x - mu
    return xc * lax.rsqrt(jnp.mean(xc * xc, axis=-1, keepdims=True) + EPS) * g + b


def _dot(a, w):
    return jnp.dot(a, w, preferred_element_type=F32)


def _convert_chunk(step, chunk_ref, dst_ref):
    rows = chunk_ref.shape[0]
    r0 = pl.multiple_of(step * rows, rows)
    dst_ref[pl.ds(r0, rows), :] = chunk_ref[...].astype(BF16)


def _chunk_spec(w):
    rows = w.shape[0] // WEIGHT_STEPS
    return pl.BlockSpec((rows, w.shape[1]), lambda i: (jnp.minimum(i, WEIGHT_STEPS - 1), 0))


def _half_swiglu(x, norm_ref, wg_ref, wu_ref, wd_ref):
    n = _rms(x, norm_ref[...]).astype(BF16)
    acc = None
    for c0, cw in FF_CHUNKS:
        g = _dot(n, wg_ref[:, c0:c0 + cw])
        u = _dot(n, wu_ref[:, c0:c0 + cw])
        d = _dot((g * jax.nn.sigmoid(g) * u).astype(BF16), wd_ref[c0:c0 + cw, :])
        acc = d if acc is None else acc + d
    return x + 0.5 * acc


def _mixer_kernel(*refs, nseg, seg_len, n_tiles, carry, start):
    h_ref, hconv_ref, hpool_ref = refs[:3]
    rest = refs[3:]
    chunks, rest = rest[:N_MIXER_MATS], rest[N_MIXER_MATS:]
    vecs, rest = rest[:N_MIXER_VECS], rest[N_MIXER_VECS:]
    outs, rest = rest[:4], rest[4:]
    mats, (cbuf, pbuf) = rest[:N_MIXER_MATS], rest[N_MIXER_MATS:]
    step = pl.program_id(0)

    @pl.when(step < WEIGHT_STEPS)
    def _():
        for chunk_ref, dst_ref in zip(chunks, mats):
            _convert_chunk(step, chunk_ref, dst_ref)

    @pl.when(step >= WEIGHT_STEPS)
    def _():
        tile = lax.rem(step - WEIGHT_STEPS, n_tiles)
        _mixer_tile(tile, h_ref, hconv_ref, hpool_ref, *mats, *vecs, *outs, cbuf, pbuf,
                    nseg=nseg, seg_len=seg_len, carry=carry, start=start)


def _mixer_tile(tile, h_ref, hconv_ref, hpool_ref,
                w_in, conv_wo, sgu_wo, pool_wo, w_out,
                mix_norm, gate_b, dw_w, dw_b, cln_g, cln_b, sln_g, sln_b, sgu_ws, sgu_bst, pool_w, pool_scale,
                h_out, conv_out, pool_out, vn_out, cbuf, pbuf, *, nseg, seg_len, carry, start):
    T = nseg * seg_len
    L = seg_len
    lane_groups = [slice(g * LANES, (g + 1) * LANES) for g in range(WA // LANES)]

    h = h_ref[...]
    n = _rms(h, mix_norm[...]).astype(BF16)

    def zin(c0, cw):
        return _dot(n, w_in[:, c0:c0 + cw])

    def load_history(s):
        for g, lanes in enumerate(lane_groups):
            cbuf[s, g, CONV_PAD - CONV_HIST:CONV_PAD, :] = hconv_ref[s, :, lanes]
            pbuf[s, g, POOL_PAD - POOL_HIST:POOL_PAD, :] = hpool_ref[s, :, lanes]

    if carry:
        pl.when(tile == 0)(functools.partial(load_history, 0))
    else:
        for s in range(nseg):
            load_history(s)

    glu = zin(0, WA) * jax.nn.sigmoid(zin(WA, WA))
    for s in range(nseg):
        for g, lanes in enumerate(lane_groups):
            cbuf[s, g, CONV_PAD:CONV_PAD + L, :] = glu[s * L:(s + 1) * L, lanes]

    def conv_block(s, r0):
        parts = []
        for g, lanes in enumerate(lane_groups):
            acc = None
            for k in range(CONV_W):
                off = CONV_PAD - CONV_HIST + k + r0
                term = dw_w[k:k + 1, lanes] * cbuf[s, g, off:off + ROW_BLOCK, :]
                acc = term if acc is None else acc + term
            parts.append(acc + dw_b[:, lanes])
        return jnp.concatenate(parts, axis=1)

    c_off = 2 * WA + 2 * WB
    g_off = c_off + WC
    gb = gate_b[...]

    def gate_part(j):
        cols = slice(j * GATE_COLS, (j + 1) * GATE_COLS)
        return jax.nn.sigmoid(zin(g_off + j * GATE_COLS, GATE_COLS) + gb[:, cols])

    mxu_tasks = ([functools.partial(zin, c_off, WC), functools.partial(zin, 2 * WA + WB, WB),
                  functools.partial(zin, 2 * WA, WB)]
                 + [functools.partial(gate_part, j) for j in range(3 * D_MODEL // GATE_COLS)])
    blocks = [(s, r0) for s in range(nseg) for r0 in range(0, L, ROW_BLOCK)]
    conv_rows, done = [], []
    for idx, (s, r0) in enumerate(blocks):
        conv_rows.append(conv_block(s, r0))
        while len(done) < (idx + 1) * len(mxu_tasks) // len(blocks):
            done.append(mxu_tasks[len(done)]())
    c, v, u = done[0], done[1], done[2]
    parts_per_gate = D_MODEL // GATE_COLS
    gates = [jnp.concatenate(done[3 + b * parts_per_gate:3 + (b + 1) * parts_per_gate], axis=1)
             for b in range(3)]

    for s in range(nseg):
        for g, lanes in enumerate(lane_groups):
            conv_out[s, :, lanes] = cbuf[s, g, L + CONV_PAD - CONV_HIST:L + CONV_PAD, :]
    if carry:
        for g in range(len(lane_groups)):
            cbuf[0, g, 0:CONV_PAD, :] = cbuf[0, g, L:L + CONV_PAD, :]

    ya = _ln(jnp.concatenate(conv_rows, axis=0), cln_g[...], cln_b[...])
    m = gates[0] * _dot((ya * jax.nn.sigmoid(ya)).astype(BF16), conv_wo[...])

    vn = _ln(v, sln_g[...], sln_b[...])
    if carry:
        vn_out[...] = vn[T - SGU_CHUNK:T]
    else:
        for s in range(nseg):
            vn_out[s] = vn[s * L:(s + 1) * L]
    rows = min(L, SGU_CHUNK)
    ri = lax.broadcasted_iota(jnp.int32, (rows, rows), 0)
    ci = lax.broadcasted_iota(jnp.int32, (rows, rows), 1)
    ws_tril = [jnp.where(ci <= ri, sgu_ws[g, 0:rows, 0:rows], 0.0) for g in range(SGU_GROUPS)]
    bst = sgu_bst[...]
    s_rows = []
    for r0 in range(0, T, rows):
        parts = []
        for g in range(SGU_GROUPS):
            sg = _dot(ws_tril[g], vn[r0:r0 + rows, g * SGU_GC:(g + 1) * SGU_GC])
            parts.append(sg + bst[0:rows, g:g + 1])
        s_rows.append(jnp.concatenate(parts, axis=1))
    m = m + gates[1] * _dot((u * jnp.concatenate(s_rows, axis=0)).astype(BF16), sgu_wo[...])

    for s in range(nseg):
        for g, lanes in enumerate(lane_groups):
            pbuf[s, g, POOL_PAD:POOL_PAD + L, :] = c[s * L:(s + 1) * L, lanes]
    pos0 = start + (tile * L if carry else 0)
    pos = pos0 + lax.broadcasted_iota(jnp.int32, (L, 1), 0)
    inv_cnt = [1.0 / jnp.minimum(pos + 1, w).astype(F32) for w in POOL_WINDOWS]
    d_rows = []
    for s in range(nseg):
        for r0 in range(0, L, ROW_BLOCK):
            parts = []
            for g, w in enumerate(POOL_WINDOWS):
                tot = None
                for j in range(w):
                    lo = POOL_PAD - j + r0
                    term = pbuf[s, g, lo:lo + ROW_BLOCK, :]
                    tot = term if tot is None else tot + term
                cur = pbuf[s, g, POOL_PAD + r0:POOL_PAD + r0 + ROW_BLOCK, :]
                parts.append(tot * inv_cnt[g][r0:r0 + ROW_BLOCK] - cur)
            d_rows.append(jnp.concatenate(parts, axis=1))
        for g, lanes in enumerate(lane_groups):
            pool_out[s, :, lanes] = pbuf[s, g, L + POOL_PAD - POOL_HIST:L + POOL_PAD, :]
    if carry:
        for g in range(len(lane_groups)):
            pbuf[0, g, 0:POOL_PAD, :] = pbuf[0, g, L:L + POOL_PAD, :]
    dd = jnp.concatenate(d_rows, axis=0)
    yc = jnp.concatenate(
        [_dot(dd[:, g * POOL_GC:(g + 1) * POOL_GC], pool_w[g]) for g in range(len(POOL_WINDOWS))], axis=1)
    m = m + gates[2] * _dot((yc * pool_scale[...]).astype(BF16), pool_wo[...])

    h_out[...] = h + _dot(m.astype(BF16), w_out[...])


def _ffn_kernel(*refs, n_prompt_tiles, with_pe, final):
    n_mats = 5 if with_pe else 3
    if with_pe:
        hp_ref, hs_ref, pp_ref, ps_ref = refs[:4]
        rest = refs[4:]
    else:
        hp_ref, hs_ref = refs[:2]
        rest = refs[2:]
    chunks, rest = rest[:n_mats], rest[n_mats:]
    mats = rest[-n_mats:]
    if with_pe:
        f_norm, pe_norm, fin_norm, op_ref, os_ref = rest[:-n_mats]
        f_wg, f_wu, f_wd, pe_wg, pe_wp = mats
    else:
        f_norm, op_ref, os_ref = rest[:-n_mats]
        f_wg, f_wu, f_wd = mats
    step = pl.program_id(0)

    @pl.when(step < WEIGHT_STEPS)
    def _():
        for chunk_ref, dst_ref in zip(chunks, mats):
            _convert_chunk(step, chunk_ref, dst_ref)

    @pl.when(step >= WEIGHT_STEPS)
    def _():
        is_prompt = step < WEIGHT_STEPS + n_prompt_tiles
        h = _half_swiglu(jnp.where(is_prompt, hp_ref[...], hs_ref[...]), f_norm, f_wg, f_wu, f_wd)
        if with_pe:
            gate = jax.nn.sigmoid(_dot(_rms(h, pe_norm[...]).astype(BF16), pe_wg[...]))
            p = jnp.where(is_prompt, pp_ref[...], ps_ref[...])
            h = h + gate * _dot(p.astype(BF16), pe_wp[...])
        if final:
            h = _rms(h, fin_norm[...])

        @pl.when(is_prompt)
        def _():
            op_ref[...] = h

        @pl.when(jnp.logical_not(is_prompt))
        def _():
            os_ref[...] = h


def _resident(arr):
    del arr
    return pl.BlockSpec(memory_space=pltpu.VMEM)


def _mixer_call(x, hconv, hpool, mats, vecs, *, nseg, seg_len, carry, start):
    assert len(mats) == N_MIXER_MATS and len(vecs) == N_MIXER_VECS
    B, S, _ = x.shape
    T = nseg * seg_len
    n_tiles = S // T
    n_streams = hconv.shape[0]

    def tile_of(i):
        j = jnp.maximum(i - WEIGHT_STEPS, 0)
        return j // n_tiles, lax.rem(j, n_tiles)

    def x_idx(i):
        b, t = tile_of(i)
        return b, t, 0

    def state_idx(i):
        b, t = tile_of(i)
        return (b if carry else t), 0, 0

    vn_rows = SGU_CHUNK if carry else seg_len
    in_specs = [
        pl.BlockSpec((None, T, D_MODEL), x_idx),
        pl.BlockSpec((nseg, CONV_HIST, WA), state_idx),
        pl.BlockSpec((nseg, POOL_HIST, WC), state_idx),
    ] + [_chunk_spec(w) for w in mats] + [_resident(w) for w in vecs]
    out_shape = (
        jax.ShapeDtypeStruct((B, S, D_MODEL), F32),
        jax.ShapeDtypeStruct((n_streams, CONV_HIST, WA), F32),
        jax.ShapeDtypeStruct((n_streams, POOL_HIST, WC), F32),
        jax.ShapeDtypeStruct((n_streams, vn_rows, WB), F32),
    )
    out_specs = (
        pl.BlockSpec((None, T, D_MODEL), x_idx),
        pl.BlockSpec((nseg, CONV_HIST, WA), state_idx),
        pl.BlockSpec((nseg, POOL_HIST, WC), state_idx),
        pl.BlockSpec((None, vn_rows, WB) if carry else (nseg, vn_rows, WB), state_idx),
    )
    return pl.pallas_call(
        functools.partial(_mixer_kernel, nseg=nseg, seg_len=seg_len, n_tiles=n_tiles, carry=carry, start=start),
        grid=(WEIGHT_STEPS + B * n_tiles,),
        in_specs=in_specs,
        out_specs=out_specs,
        out_shape=out_shape,
        scratch_shapes=[pltpu.VMEM(w.shape, BF16) for w in mats] + [
            pltpu.VMEM((nseg, WA // LANES, CONV_PAD + seg_len, LANES), F32),
            pltpu.VMEM((nseg, WC // LANES, POOL_PAD + seg_len, LANES), F32),
        ],
        compiler_params=pltpu.CompilerParams(
            dimension_semantics=("arbitrary",),
            vmem_limit_bytes=V7X_VMEM_LIMIT_BYTES),
        name="mixer_carry" if carry else "mixer_streams",
    )(x, hconv, hpool, *mats, *vecs)


def _ffn_call(hp, hs, pe_inputs, big_w, wts, *, final):
    n_p = hp.shape[0] // FFN_TILE
    n_s = hs.shape[0] // FFN_TILE
    prompt_idx = lambda i: (jnp.clip(i - WEIGHT_STEPS, 0, n_p - 1), 0)
    sample_idx = lambda i: (jnp.clip(i - WEIGHT_STEPS - n_p, 0, n_s - 1), 0)
    tok_specs = [pl.BlockSpec((FFN_TILE, D_MODEL), prompt_idx), pl.BlockSpec((FFN_TILE, D_MODEL), sample_idx)]
    pe_specs = [pl.BlockSpec((FFN_TILE, D_PLE), prompt_idx), pl.BlockSpec((FFN_TILE, D_PLE), sample_idx)]
    with_pe = bool(pe_inputs)
    return pl.pallas_call(
        functools.partial(_ffn_kernel, n_prompt_tiles=n_p, with_pe=with_pe, final=final),
        grid=(WEIGHT_STEPS + n_p + n_s,),
        in_specs=(tok_specs + (pe_specs if with_pe else []) + [_chunk_spec(w) for w in big_w]
                  + [_resident(w) for w in wts]),
        out_specs=tuple(tok_specs),
        out_shape=(jax.ShapeDtypeStruct(hp.shape, F32), jax.ShapeDtypeStruct(hs.shape, F32)),
        scratch_shapes=[pltpu.VMEM(w.shape, BF16) for w in big_w],
        compiler_params=pltpu.CompilerParams(
            dimension_semantics=("arbitrary",),
            vmem_limit_bytes=V7X_VMEM_LIMIT_BYTES),
        name="ffn_pe" if with_pe else "ffn",
    )(hp, hs, *pe_inputs, *big_w, *wts)


def kernel(x_prompt, x_sample, p_prompt, p_sample, cache_conv, cache_pool, ffn1_norm, ffn1_w_gate, ffn1_w_up, ffn1_w_down, mix_norm, w_in, gate_b, conv_dw_w, conv_dw_b, conv_ln_g, conv_ln_b, conv_w_out, sgu_ln_g, sgu_ln_b, sgu_w_s, sgu_b_s, sgu_w_out, pool_w, pool_scale, pool_w_out, w_out, ffn2_norm, ffn2_w_gate, ffn2_w_up, ffn2_w_down, pe_norm, pe_w_gate, pe_w_proj, final_norm):
    depth = ffn1_norm.shape[0]
    bp, sp, _ = x_prompt.shape
    bs, ss, _ = x_sample.shape
    row = lambda v: v.reshape(1, -1)

    hp = x_prompt.reshape(bp * sp, D_MODEL)
    hs = x_sample.reshape(bs * ss, D_MODEL)
    conv_zero = jnp.zeros((bp, CONV_HIST, WA), F32)
    pool_zero = jnp.zeros((bp, POOL_HIST, WC), F32)
    fin = row(final_norm)
    conv_p, conv_s, pool_p, pool_s, v_p, v_s = [], [], [], [], [], []
    for i in range(depth):
        mixer_mats = (w_in[i], conv_w_out[i], sgu_w_out[i], pool_w_out[i], w_out[i])
        mixer_vecs = (
            row(mix_norm[i]), row(gate_b[i]),
            conv_dw_w[i], row(conv_dw_b[i]), row(conv_ln_g[i]), row(conv_ln_b[i]),
            row(sgu_ln_g[i]), row(sgu_ln_b[i]), sgu_w_s[i], sgu_b_s[i].T,
            pool_w[i], row(pool_scale[i]),
        )
        hp, hs = _ffn_call(hp, hs, (), (ffn1_w_gate[i], ffn1_w_up[i], ffn1_w_down[i]),
                           (row(ffn1_norm[i]),), final=False)
        hp, cp, pp, vp = _mixer_call(hp.reshape(bp, sp, D_MODEL), conv_zero, pool_zero, mixer_mats, mixer_vecs,
                                     nseg=1, seg_len=PROMPT_TILE, carry=True, start=0)
        hs, cs, ps, vs = _mixer_call(hs.reshape(1, bs * ss, D_MODEL), cache_conv[i], cache_pool[i],
                                     mixer_mats, mixer_vecs, nseg=SAMPLE_STREAMS_PER_TILE, seg_len=ss,
                                     carry=False, start=PAST_LEN)
        hp, hs = _ffn_call(hp.reshape(bp * sp, D_MODEL), hs.reshape(bs * ss, D_MODEL),
                           (p_prompt[i].reshape(bp * sp, D_PLE), p_sample[i].reshape(bs * ss, D_PLE)),
                           (ffn2_w_gate[i], ffn2_w_up[i], ffn2_w_down[i], pe_w_gate[i], pe_w_proj[i]),
                           (row(ffn2_norm[i]), row(pe_norm[i]), fin),
                           final=i == depth - 1)

        conv_p.append(cp); conv_s.append(cs); pool_p.append(pp); pool_s.append(ps)
        v_p.append(vp); v_s.append(vs)
    return (hp.reshape(bp, sp, D_MODEL), hs.reshape(bs, ss, D_MODEL), jnp.stack(conv_p), jnp.stack(conv_s),
            jnp.stack(pool_p), jnp.stack(pool_s), jnp.stack(v_p), jnp.stack(v_s))
```

```python
import functools

import jax
import jax.numpy as jnp
from jax import lax
from jax.experimental import pallas as pl
from jax.experimental.pallas import tpu as pltpu

D_MODEL = 1024
D_PLE = 256
D_FF = 2816
WA = 512
CONV_W = 31
CONV_HIST = CONV_W - 1
WB = 512
SGU_GROUPS = 4
SGU_CHUNK = 128
SGU_GC = WB // SGU_GROUPS
WC = 512
POOL_WINDOWS = (2, 4, 8, 16)
POOL_GC = WC // len(POOL_WINDOWS)
POOL_HIST = max(POOL_WINDOWS) - 1
D_IN = 2 * WA + 2 * WB + WC + 3 * D_MODEL
PAST_LEN = 2048
EPS = 1e-6

F32 = jnp.float32
BF16 = jnp.bfloat16

V7X_VMEM_LIMIT_BYTES = 60 * 1024 * 1024
LANES = 128
CONV_PAD = 32
POOL_PAD = 16
ROW_BLOCK = 32
GATE_COLS = 512
FF_CHUNKS = ((0, 1024), (1024, 1024), (2048, 768))
WEIGHT_STEPS = 8
N_MIXER_MATS = 5
N_MIXER_VECS = 12

PROMPT_TILE = 256
SAMPLE_STREAMS_PER_TILE = 4
FFN_TILE = 256


def _rms(x, g):
    return x * lax.rsqrt(jnp.mean(x * x, axis=-1, keepdims=True) + EPS) * g


def _ln(x, g, b):
    mu = jnp.mean(x, axis=-1, keepdims=True)
    xc = x - mu
    return xc * lax.rsqrt(jnp.mean(xc * xc, axis=-1, keepdims=True) + EPS) * g + b


def _dot(a, w):
    return jnp.dot(a, w, preferred_element_type=F32)


def _convert_chunk(step, chunk_ref, dst_ref):
    rows = chunk_ref.shape[0]
    r0 = pl.multiple_of(step * rows, rows)
    dst_ref[pl.ds(r0, rows), :] = chunk_ref[...].astype(BF16)


def _chunk_spec(w, layer):
    rows = w.shape[1] // WEIGHT_STEPS
    return pl.BlockSpec((None, rows, w.shape[2]), lambda i: (layer, jnp.minimum(i, WEIGHT_STEPS - 1), 0))


def _half_swiglu(x, norm_ref, wg_ref, wu_ref, wd_ref):
    n = _rms(x, norm_ref[...]).astype(BF16)
    acc = None
    for c0, cw in FF_CHUNKS:
        g = _dot(n, wg_ref[:, c0:c0 + cw])
        u = _dot(n, wu_ref[:, c0:c0 + cw])
        d = _dot((g * jax.nn.sigmoid(g) * u).astype(BF16), wd_ref[c0:c0 + cw, :])
        acc = d if acc is None else acc + d
    return x + 0.5 * acc


def _mixer_kernel(*refs, layer, nseg, seg_len, n_tiles, carry, start):
    h_ref, hconv_ref, hpool_ref = refs[:3]
    rest = refs[3:]
    chunks, rest = rest[:N_MIXER_MATS], rest[N_MIXER_MATS:]
    vecs, rest = tuple(v.at[layer] for v in rest[:N_MIXER_VECS]), rest[N_MIXER_VECS:]
    outs, rest = rest[:4], rest[4:]
    mats, (cbuf, pbuf) = rest[:N_MIXER_MATS], rest[N_MIXER_MATS:]
    step = pl.program_id(0)

    @pl.when(step < WEIGHT_STEPS)
    def _():
        for chunk_ref, dst_ref in zip(chunks, mats):
            _convert_chunk(step, chunk_ref, dst_ref)

    @pl.when(step >= WEIGHT_STEPS)
    def _():
        tile = lax.rem(step - WEIGHT_STEPS, n_tiles)
        _mixer_tile(tile, h_ref, hconv_ref, hpool_ref, *mats, *vecs, *outs, cbuf, pbuf,
                    nseg=nseg, seg_len=seg_len, carry=carry, start=start)


def _mixer_tile(tile, h_ref, hconv_ref, hpool_ref,
                w_in, conv_wo, sgu_wo, pool_wo, w_out,
                mix_norm, gate_b, dw_w, dw_b, cln_g, cln_b, sln_g, sln_b, sgu_ws, sgu_bst, pool_w, pool_scale,
                h_out, conv_out, pool_out, vn_out, cbuf, pbuf, *, nseg, seg_len, carry, start):
    T = nseg * seg_len
    L = seg_len
    lane_groups = [slice(g * LANES, (g + 1) * LANES) for g in range(WA // LANES)]

    h = h_ref[...]
    n = _rms(h, mix_norm[...]).astype(BF16)

    def zin(c0, cw):
        return _dot(n, w_in[:, c0:c0 + cw])

    def load_history(s):
        for g, lanes in enumerate(lane_groups):
            cbuf[s, g, CONV_PAD - CONV_HIST:CONV_PAD, :] = hconv_ref[s, :, lanes]
            pbuf[s, g, POOL_PAD - POOL_HIST:POOL_PAD, :] = hpool_ref[s, :, lanes]

    if carry:
        pl.when(tile == 0)(functools.partial(load_history, 0))
    else:
        for s in range(nseg):
            load_history(s)

    glu = zin(0, WA) * jax.nn.sigmoid(zin(WA, WA))
    for s in range(nseg):
        for g, lanes in enumerate(lane_groups):
            cbuf[s, g, CONV_PAD:CONV_PAD + L, :] = glu[s * L:(s + 1) * L, lanes]

    def conv_block(s, r0):
        parts = []
        for g, lanes in enumerate(lane_groups):
            acc = None
            for k in range(CONV_W):
                off = CONV_PAD - CONV_HIST + k + r0
                term = dw_w[k:k + 1, lanes] * cbuf[s, g, off:off + ROW_BLOCK, :]
                acc = term if acc is None else acc + term
            parts.append(acc + dw_b[:, lanes])
        return jnp.concatenate(parts, axis=1)

    c_off = 2 * WA + 2 * WB
    g_off = c_off + WC
    gb = gate_b[...]

    def gate_part(j):
        cols = slice(j * GATE_COLS, (j + 1) * GATE_COLS)
        return jax.nn.sigmoid(zin(g_off + j * GATE_COLS, GATE_COLS) + gb[:, cols])

    mxu_tasks = ([functools.partial(zin, c_off, WC), functools.partial(zin, 2 * WA + WB, WB),
                  functools.partial(zin, 2 * WA, WB)]
                 + [functools.partial(gate_part, j) for j in range(3 * D_MODEL // GATE_COLS)])
    blocks = [(s, r0) for s in range(nseg) for r0 in range(0, L, ROW_BLOCK)]
    conv_rows, done = [], []
    for idx, (s, r0) in enumerate(blocks):
        conv_rows.append(conv_block(s, r0))
        while len(done) < (idx + 1) * len(mxu_tasks) // len(blocks):
            done.append(mxu_tasks[len(done)]())
    c, v, u = done[0], done[1], done[2]
    parts_per_gate = D_MODEL // GATE_COLS
    gates = [jnp.concatenate(done[3 + b * parts_per_gate:3 + (b + 1) * parts_per_gate], axis=1)
             for b in range(3)]

    for s in range(nseg):
        for g, lanes in enumerate(lane_groups):
            conv_out[s, :, lanes] = cbuf[s, g, L + CONV_PAD - CONV_HIST:L + CONV_PAD, :]
    if carry:
        for g in range(len(lane_groups)):
            cbuf[0, g, 0:CONV_PAD, :] = cbuf[0, g, L:L + CONV_PAD, :]

    ya = _ln(jnp.concatenate(conv_rows, axis=0), cln_g[...], cln_b[...])
    m = gates[0] * _dot((ya * jax.nn.sigmoid(ya)).astype(BF16), conv_wo[...])

    vn = _ln(v, sln_g[...], sln_b[...])
    if carry:
        vn_out[...] = vn[T - SGU_CHUNK:T]
    else:
        for s in range(nseg):
            vn_out[s] = vn[s * L:(s + 1) * L]
    rows = min(L, SGU_CHUNK)
    ri = lax.broadcasted_iota(jnp.int32, (rows, rows), 0)
    ci = lax.broadcasted_iota(jnp.int32, (rows, rows), 1)
    ws_tril = [jnp.where(ci <= ri, sgu_ws[g, 0:rows, 0:rows], 0.0) for g in range(SGU_GROUPS)]
    bst = sgu_bst[...]
    s_rows = []
    for r0 in range(0, T, rows):
        parts = []
        for g in range(SGU_GROUPS):
            sg = _dot(ws_tril[g], vn[r0:r0 + rows, g * SGU_GC:(g + 1) * SGU_GC])
            parts.append(sg + bst[0:rows, g:g + 1])
        s_rows.append(jnp.concatenate(parts, axis=1))
    m = m + gates[1] * _dot((u * jnp.concatenate(s_rows, axis=0)).astype(BF16), sgu_wo[...])

    for s in range(nseg):
        for g, lanes in enumerate(lane_groups):
            pbuf[s, g, POOL_PAD:POOL_PAD + L, :] = c[s * L:(s + 1) * L, lanes]
    pos0 = start + (tile * L if carry else 0)
    pos = pos0 + lax.broadcasted_iota(jnp.int32, (L, 1), 0)
    inv_cnt = [1.0 / jnp.minimum(pos + 1, w).astype(F32) for w in POOL_WINDOWS]
    d_rows = []
    for s in range(nseg):
        for r0 in range(0, L, ROW_BLOCK):
            parts = []
            for g, w in enumerate(POOL_WINDOWS):
                tot = None
                for j in range(w):
                    lo = POOL_PAD - j + r0
                    term = pbuf[s, g, lo:lo + ROW_BLOCK, :]
                    tot = term if tot is None else tot + term
                cur = pbuf[s, g, POOL_PAD + r0:POOL_PAD + r0 + ROW_BLOCK, :]
                parts.append(tot * inv_cnt[g][r0:r0 + ROW_BLOCK] - cur)
            d_rows.append(jnp.concatenate(parts, axis=1))
        for g, lanes in enumerate(lane_groups):
            pool_out[s, :, lanes] = pbuf[s, g, L + POOL_PAD - POOL_HIST:L + POOL_PAD, :]
    if carry:
        for g in range(len(lane_groups)):
            pbuf[0, g, 0:POOL_PAD, :] = pbuf[0, g, L:L + POOL_PAD, :]
    dd = jnp.concatenate(d_rows, axis=0)
    yc = jnp.concatenate(
        [_dot(dd[:, g * POOL_GC:(g + 1) * POOL_GC], pool_w[g]) for g in range(len(POOL_WINDOWS))], axis=1)
    m = m + gates[2] * _dot((yc * pool_scale[...]).astype(BF16), pool_wo[...])

    h_out[...] = h + _dot(m.astype(BF16), w_out[...])


def _ffn_kernel(*refs, layer, n_prompt_tiles, with_pe, final):
    n_mats = 5 if with_pe else 3
    if with_pe:
        hp_ref, hs_ref, pp_ref, ps_ref = refs[:4]
        rest = refs[4:]
    else:
        hp_ref, hs_ref = refs[:2]
        rest = refs[2:]
    chunks, rest = rest[:n_mats], rest[n_mats:]
    mats = rest[-n_mats:]
    if with_pe:
        f_norm, pe_norm, fin_norm, op_ref, os_ref = rest[:-n_mats]
        f_wg, f_wu, f_wd, pe_wg, pe_wp = mats
        pe_norm = pe_norm.at[layer]
    else:
        f_norm, op_ref, os_ref = rest[:-n_mats]
        f_wg, f_wu, f_wd = mats
    f_norm = f_norm.at[layer]
    step = pl.program_id(0)

    @pl.when(step < WEIGHT_STEPS)
    def _():
        for chunk_ref, dst_ref in zip(chunks, mats):
            _convert_chunk(step, chunk_ref, dst_ref)

    @pl.when(step >= WEIGHT_STEPS)
    def _():
        is_prompt = step < WEIGHT_STEPS + n_prompt_tiles
        h = _half_swiglu(jnp.where(is_prompt, hp_ref[...], hs_ref[...]), f_norm, f_wg, f_wu, f_wd)
        if with_pe:
            gate = jax.nn.sigmoid(_dot(_rms(h, pe_norm[...]).astype(BF16), pe_wg[...]))
            p = jnp.where(is_prompt, pp_ref[...], ps_ref[...])
            h = h + gate * _dot(p.astype(BF16), pe_wp[...])
        if final:
            h = _rms(h, fin_norm[...])

        @pl.when(is_prompt)
        def _():
            op_ref[...] = h

        @pl.when(jnp.logical_not(is_prompt))
        def _():
            os_ref[...] = h


def _resident(arr):
    del arr
    return pl.BlockSpec(memory_space=pltpu.VMEM)


def _mixer_call(x, hconv, hpool, mats, vecs, *, layer, hist_layer, nseg, seg_len, carry, start):
    assert len(mats) == N_MIXER_MATS and len(vecs) == N_MIXER_VECS
    B, S, _ = x.shape
    T = nseg * seg_len
    n_tiles = S // T
    n_streams = hconv.shape[1]

    def tile_of(i):
        j = jnp.maximum(i - WEIGHT_STEPS, 0)
        return j // n_tiles, lax.rem(j, n_tiles)

    def x_idx(i):
        b, t = tile_of(i)
        return b, t, 0

    def state_idx(i):
        b, t = tile_of(i)
        return (b if carry else t), 0, 0

    def hist_idx(i):
        return (hist_layer,) + state_idx(i)

    vn_rows = SGU_CHUNK if carry else seg_len
    in_specs = [
        pl.BlockSpec((None, T, D_MODEL), x_idx),
        pl.BlockSpec((None, nseg, CONV_HIST, WA), hist_idx),
        pl.BlockSpec((None, nseg, POOL_HIST, WC), hist_idx),
    ] + [_chunk_spec(w, layer) for w in mats] + [_resident(w) for w in vecs]
    out_shape = (
        jax.ShapeDtypeStruct((B, S, D_MODEL), F32),
        jax.ShapeDtypeStruct((n_streams, CONV_HIST, WA), F32),
        jax.ShapeDtypeStruct((n_streams, POOL_HIST, WC), F32),
        jax.ShapeDtypeStruct((n_streams, vn_rows, WB), F32),
    )
    out_specs = (
        pl.BlockSpec((None, T, D_MODEL), x_idx),
        pl.BlockSpec((nseg, CONV_HIST, WA), state_idx),
        pl.BlockSpec((nseg, POOL_HIST, WC), state_idx),
        pl.BlockSpec((None, vn_rows, WB) if carry else (nseg, vn_rows, WB), state_idx),
    )
    return pl.pallas_call(
        functools.partial(_mixer_kernel, layer=layer, nseg=nseg, seg_len=seg_len, n_tiles=n_tiles, carry=carry,
                          start=start),
        grid=(WEIGHT_STEPS + B * n_tiles,),
        in_specs=in_specs,
        out_specs=out_specs,
        out_shape=out_shape,
        scratch_shapes=[pltpu.VMEM(w.shape[1:], BF16) for w in mats] + [
            pltpu.VMEM((nseg, WA // LANES, CONV_PAD + seg_len, LANES), F32),
            pltpu.VMEM((nseg, WC // LANES, POOL_PAD + seg_len, LANES), F32),
        ],
        compiler_params=pltpu.CompilerParams(
            dimension_semantics=("arbitrary",),
            vmem_limit_bytes=V7X_VMEM_LIMIT_BYTES),
        name="mixer_carry" if carry else "mixer_streams",
    )(x, hconv, hpool, *mats, *vecs)


def _ffn_call(hp, hs, pe_inputs, big_w, wts, *, layer, final):
    n_p = hp.shape[0] // FFN_TILE
    n_s = hs.shape[0] // FFN_TILE
    prompt_idx = lambda i: (jnp.clip(i - WEIGHT_STEPS, 0, n_p - 1), 0)
    sample_idx = lambda i: (jnp.clip(i - WEIGHT_STEPS - n_p, 0, n_s - 1), 0)
    tok_specs = [pl.BlockSpec((FFN_TILE, D_MODEL), prompt_idx), pl.BlockSpec((FFN_TILE, D_MODEL), sample_idx)]
    pe_specs = [pl.BlockSpec((None, FFN_TILE, D_PLE), lambda i: (layer,) + prompt_idx(i)),
                pl.BlockSpec((None, FFN_TILE, D_PLE), lambda i: (layer,) + sample_idx(i))]
    with_pe = bool(pe_inputs)
    return pl.pallas_call(
        functools.partial(_ffn_kernel, layer=layer, n_prompt_tiles=n_p, with_pe=with_pe, final=final),
        grid=(WEIGHT_STEPS + n_p + n_s,),
        in_specs=(tok_specs + (pe_specs if with_pe else []) + [_chunk_spec(w, layer) for w in big_w]
                  + [_resident(w) for w in wts]),
        out_specs=tuple(tok_specs),
        out_shape=(jax.ShapeDtypeStruct(hp.shape, F32), jax.ShapeDtypeStruct(hs.shape, F32)),
        scratch_shapes=[pltpu.VMEM(w.shape[1:], BF16) for w in big_w],
        compiler_params=pltpu.CompilerParams(
            dimension_semantics=("arbitrary",),
            vmem_limit_bytes=V7X_VMEM_LIMIT_BYTES),
        name="ffn_pe" if with_pe else "ffn",
    )(hp, hs, *pe_inputs, *big_w, *wts)


def kernel(x_prompt, x_sample, p_prompt, p_sample, cache_conv, cache_pool, ffn1_norm, ffn1_w_gate, ffn1_w_up, ffn1_w_down, mix_norm, w_in, gate_b, conv_dw_w, conv_dw_b, conv_ln_g, conv_ln_b, conv_w_out, sgu_ln_g, sgu_ln_b, sgu_w_s, sgu_b_s, sgu_w_out, pool_w, pool_scale, pool_w_out, w_out, ffn2_norm, ffn2_w_gate, ffn2_w_up, ffn2_w_down, pe_norm, pe_w_gate, pe_w_proj, final_norm):
    depth = ffn1_norm.shape[0]
    bp, sp, _ = x_prompt.shape
    bs, ss, _ = x_sample.shape
    rows = lambda v: v.reshape(v.shape[0], 1, -1)

    hp = x_prompt.reshape(bp * sp, D_MODEL)
    hs = x_sample.reshape(bs * ss, D_MODEL)
    pp_all = p_prompt.reshape(depth, bp * sp, D_PLE)
    ps_all = p_sample.reshape(depth, bs * ss, D_PLE)
    conv_zero = jnp.zeros((1, bp, CONV_HIST, WA), F32)
    pool_zero = jnp.zeros((1, bp, POOL_HIST, WC), F32)
    ffn1_mats = (ffn1_w_gate, ffn1_w_up, ffn1_w_down)
    ffn1_vecs = (rows(ffn1_norm),)
    mixer_mats = (w_in, conv_w_out, sgu_w_out, pool_w_out, w_out)
    mixer_vecs = (
        rows(mix_norm), rows(gate_b),
        conv_dw_w, rows(conv_dw_b), rows(conv_ln_g), rows(conv_ln_b),
        rows(sgu_ln_g), rows(sgu_ln_b), sgu_w_s, jnp.swapaxes(sgu_b_s, 1, 2),
        pool_w, rows(pool_scale),
    )
    ffn2_mats = (ffn2_w_gate, ffn2_w_up, ffn2_w_down, pe_w_gate, pe_w_proj)
    ffn2_vecs = (rows(ffn2_norm), rows(pe_norm), final_norm.reshape(1, -1))
    conv_p, conv_s, pool_p, pool_s, v_p, v_s = [], [], [], [], [], []
    for i in range(depth):
        hp, hs = _ffn_call(hp, hs, (), ffn1_mats, ffn1_vecs, layer=i, final=False)
        hp, cp, pp, vp = _mixer_call(hp.reshape(bp, sp, D_MODEL), conv_zero, pool_zero, mixer_mats, mixer_vecs,
                                     layer=i, hist_layer=0, nseg=1, seg_len=PROMPT_TILE, carry=True, start=0)
        hs, cs, ps, vs = _mixer_call(hs.reshape(1, bs * ss, D_MODEL), cache_conv, cache_pool,
                                     mixer_mats, mixer_vecs, layer=i, hist_layer=i,
                                     nseg=SAMPLE_STREAMS_PER_TILE, seg_len=ss, carry=False, start=PAST_LEN)
        hp, hs = _ffn_call(hp.reshape(bp * sp, D_MODEL), hs.reshape(bs * ss, D_MODEL), (pp_all, ps_all),
                           ffn2_mats, ffn2_vecs, layer=i, final=i == depth - 1)

        conv_p.append(cp); conv_s.append(cs); pool_p.append(pp); pool_s.append(ps)
        v_p.append(vp); v_s.append(vs)
    return (hp.reshape(bp, sp, D_MODEL), hs.reshape(bs, ss, D_MODEL), jnp.stack(conv_p), jnp.stack(conv_s),
            jnp.stack(pool_p), jnp.stack(pool_s), jnp.stack(v_p), jnp.stack(v_s))
```

```python
import functools

import jax
import jax.numpy as jnp
from jax import lax
from jax.experimental import pallas as pl
from jax.experimental.pallas import tpu as pltpu

D_MODEL = 1024
D_PLE = 256
D_FF = 2816
WA = 512
CONV_W = 31
CONV_HIST = CONV_W - 1
WB = 512
SGU_GROUPS = 4
SGU_CHUNK = 128
SGU_GC = WB // SGU_GROUPS
WC = 512
POOL_WINDOWS = (2, 4, 8, 16)
POOL_GC = WC // len(POOL_WINDOWS)
POOL_HIST = max(POOL_WINDOWS) - 1
D_IN = 2 * WA + 2 * WB + WC + 3 * D_MODEL
PAST_LEN = 2048
EPS = 1e-6

F32 = jnp.float32
BF16 = jnp.bfloat16

V7X_VMEM_LIMIT_BYTES = 60 * 1024 * 1024
LANES = 128
CONV_PAD = 32
POOL_PAD = 16
ROW_BLOCK = 16
GATE_COLS = 512
FF_CHUNKS = ((0, 1024), (1024, 1024), (2048, 768))
WEIGHT_STEPS = 8
N_MIXER_MATS = 5
N_MIXER_VECS = 12

PROMPT_TILE = 256
SAMPLE_STREAMS_PER_TILE = 4
FFN_TILE = 256


def _rms(x, g):
    return x * lax.rsqrt(jnp.mean(x * x, axis=-1, keepdims=True) + EPS) * g


def _ln(x, g, b):
    mu = jnp.mean(x, axis=-1, keepdims=True)
    xc = x - mu
    return xc * lax.rsqrt(jnp.mean(xc * xc, axis=-1, keepdims=True) + EPS) * g + b


def _dot(a, w):
    return jnp.dot(a, w, preferred_element_type=F32)


def _convert_chunk(step, chunk_ref, dst_ref):
    rows = chunk_ref.shape[0]
    r0 = pl.multiple_of(step * rows, rows)
    dst_ref[pl.ds(r0, rows), :] = chunk_ref[...].astype(BF16)


def _layer_view(ref, layer):
    return ref.at[pl.ds(layer, 1)] if len(ref.shape) == 2 else ref.at[layer]


def _chunk_spec(w, layer):
    rows = w.shape[1] // WEIGHT_STEPS
    return pl.BlockSpec((None, rows, w.shape[2]), lambda i: (layer, jnp.minimum(i, WEIGHT_STEPS - 1), 0))


def _half_swiglu(x, norm_ref, wg_ref, wu_ref, wd_ref):
    n = _rms(x, norm_ref[...]).astype(BF16)
    acc = None
    for c0, cw in FF_CHUNKS:
        g = _dot(n, wg_ref[:, c0:c0 + cw])
        u = _dot(n, wu_ref[:, c0:c0 + cw])
        d = _dot((g * jax.nn.sigmoid(g) * u).astype(BF16), wd_ref[c0:c0 + cw, :])
        acc = d if acc is None else acc + d
    return x + 0.5 * acc


def _mixer_kernel(*refs, layer, nseg, seg_len, n_tiles, carry, start):
    h_ref, hconv_ref, hpool_ref = refs[:3]
    rest = refs[3:]
    chunks, rest = rest[:N_MIXER_MATS], rest[N_MIXER_MATS:]
    vecs, rest = tuple(_layer_view(v, layer) for v in rest[:N_MIXER_VECS]), rest[N_MIXER_VECS:]
    outs, rest = rest[:4], rest[4:]
    mats, (cbuf, pbuf) = rest[:N_MIXER_MATS], rest[N_MIXER_MATS:]
    step = pl.program_id(0)

    @pl.when(step < WEIGHT_STEPS)
    def _():
        for chunk_ref, dst_ref in zip(chunks, mats):
            _convert_chunk(step, chunk_ref, dst_ref)

    @pl.when(step >= WEIGHT_STEPS)
    def _():
        tile = lax.rem(step - WEIGHT_STEPS, n_tiles)
        _mixer_tile(tile, h_ref, hconv_ref, hpool_ref, *mats, *vecs, *outs, cbuf, pbuf,
                    nseg=nseg, seg_len=seg_len, carry=carry, start=start)


def _mixer_tile(tile, h_ref, hconv_ref, hpool_ref,
                w_in, conv_wo, sgu_wo, pool_wo, w_out,
                mix_norm, gate_b, dw_w, dw_b, cln_g, cln_b, sln_g, sln_b, sgu_ws, sgu_bst, pool_w, pool_scale,
                h_out, conv_out, pool_out, vn_out, cbuf, pbuf, *, nseg, seg_len, carry, start):
    T = nseg * seg_len
    L = seg_len
    lane_groups = [slice(g * LANES, (g + 1) * LANES) for g in range(WA // LANES)]

    h = h_ref[...]
    n = _rms(h, mix_norm[...]).astype(BF16)

    def zin(c0, cw):
        return _dot(n, w_in[:, c0:c0 + cw])

    def load_history(s):
        for g, lanes in enumerate(lane_groups):
            cbuf[s, g, CONV_PAD - CONV_HIST:CONV_PAD, :] = hconv_ref[s, :, lanes]
            pbuf[s, g, POOL_PAD - POOL_HIST:POOL_PAD, :] = hpool_ref[s, :, lanes]

    if carry:
        pl.when(tile == 0)(functools.partial(load_history, 0))
    else:
        for s in range(nseg):
            load_history(s)

    glu = zin(0, WA) * jax.nn.sigmoid(zin(WA, WA))
    for s in range(nseg):
        for g, lanes in enumerate(lane_groups):
            cbuf[s, g, CONV_PAD:CONV_PAD + L, :] = glu[s * L:(s + 1) * L, lanes]

    never = tile < 0
    prev_chain = [None]

    def conv_block(s, r0):
        parts = []
        for g, lanes in enumerate(lane_groups):
            acc = None
            for k in range(CONV_W):
                off = CONV_PAD - CONV_HIST + k + r0
                term = dw_w[k:k + 1, lanes] * cbuf[s, g, off:off + ROW_BLOCK, :]
                if acc is not None:
                    acc = acc + term
                elif prev_chain[0] is None:
                    acc = term
                else:
                    acc = jnp.where(never, prev_chain[0], term)
            prev_chain[0] = acc
            parts.append(acc + dw_b[:, lanes])
        return jnp.concatenate(parts, axis=1)

    c_off = 2 * WA + 2 * WB
    g_off = c_off + WC
    gb = gate_b[...]

    def gate_part(j):
        cols = slice(j * GATE_COLS, (j + 1) * GATE_COLS)
        return jax.nn.sigmoid(zin(g_off + j * GATE_COLS, GATE_COLS) + gb[:, cols])

    mxu_tasks = ([functools.partial(zin, c_off, WC), functools.partial(zin, 2 * WA + WB, WB),
                  functools.partial(zin, 2 * WA, WB)]
                 + [functools.partial(gate_part, j) for j in range(3 * D_MODEL // GATE_COLS)])
    blocks = [(s, r0) for s in range(nseg) for r0 in range(0, L, ROW_BLOCK)]
    conv_rows, done = [], []
    for idx, (s, r0) in enumerate(blocks):
        conv_rows.append(conv_block(s, r0))
        while len(done) < (idx + 1) * len(mxu_tasks) // len(blocks):
            done.append(mxu_tasks[len(done)]())
    c, v, u = done[0], done[1], done[2]
    parts_per_gate = D_MODEL // GATE_COLS
    gates = [jnp.concatenate(done[3 + b * parts_per_gate:3 + (b + 1) * parts_per_gate], axis=1)
             for b in range(3)]

    for s in range(nseg):
        for g, lanes in enumerate(lane_groups):
            conv_out[s, :, lanes] = cbuf[s, g, L + CONV_PAD - CONV_HIST:L + CONV_PAD, :]
    if carry:
        for g in range(len(lane_groups)):
            cbuf[0, g, 0:CONV_PAD, :] = cbuf[0, g, L:L + CONV_PAD, :]

    ya = _ln(jnp.concatenate(conv_rows, axis=0), cln_g[...], cln_b[...])
    m = gates[0] * _dot((ya * jax.nn.sigmoid(ya)).astype(BF16), conv_wo[...])

    vn = _ln(v, sln_g[...], sln_b[...])
    if carry:
        vn_out[...] = vn[T - SGU_CHUNK:T]
    else:
        for s in range(nseg):
            vn_out[s] = vn[s * L:(s + 1) * L]
    rows = min(L, SGU_CHUNK)
    ri = lax.broadcasted_iota(jnp.int32, (rows, rows), 0)
    ci = lax.broadcasted_iota(jnp.int32, (rows, rows), 1)
    ws_tril = [jnp.where(ci <= ri, sgu_ws[g, 0:rows, 0:rows], 0.0) for g in range(SGU_GROUPS)]
    bst = sgu_bst[...]
    s_rows = []
    for r0 in range(0, T, rows):
        parts = []
        for g in range(SGU_GROUPS):
            sg = _dot(ws_tril[g], vn[r0:r0 + rows, g * SGU_GC:(g + 1) * SGU_GC])
            parts.append(sg + bst[0:rows, g:g + 1])
        s_rows.append(jnp.concatenate(parts, axis=1))
    m = m + gates[1] * _dot((u * jnp.concatenate(s_rows, axis=0)).astype(BF16), sgu_wo[...])

    for s in range(nseg):
        for g, lanes in enumerate(lane_groups):
            pbuf[s, g, POOL_PAD:POOL_PAD + L, :] = c[s * L:(s + 1) * L, lanes]
    pos0 = start + (tile * L if carry else 0)
    pos = pos0 + lax.broadcasted_iota(jnp.int32, (L, 1), 0)
    inv_cnt = [1.0 / jnp.minimum(pos + 1, w).astype(F32) for w in POOL_WINDOWS]
    d_rows = []
    for s in range(nseg):
        for r0 in range(0, L, ROW_BLOCK):
            parts = []
            for g, w in enumerate(POOL_WINDOWS):
                tot = None
                for j in range(w):
                    lo = POOL_PAD - j + r0
                    term = pbuf[s, g, lo:lo + ROW_BLOCK, :]
                    tot = term if tot is None else tot + term
                cur = pbuf[s, g, POOL_PAD + r0:POOL_PAD + r0 + ROW_BLOCK, :]
                parts.append(tot * inv_cnt[g][r0:r0 + ROW_BLOCK] - cur)
            d_rows.append(jnp.concatenate(parts, axis=1))
        for g, lanes in enumerate(lane_groups):
            pool_out[s, :, lanes] = pbuf[s, g, L + POOL_PAD - POOL_HIST:L + POOL_PAD, :]
    if carry:
        for g in range(len(lane_groups)):
            pbuf[0, g, 0:POOL_PAD, :] = pbuf[0, g, L:L + POOL_PAD, :]
    dd = jnp.concatenate(d_rows, axis=0)
    yc = jnp.concatenate(
        [_dot(dd[:, g * POOL_GC:(g + 1) * POOL_GC], pool_w[g]) for g in range(len(POOL_WINDOWS))], axis=1)
    m = m + gates[2] * _dot((yc * pool_scale[...]).astype(BF16), pool_wo[...])

    h_out[...] = h + _dot(m.astype(BF16), w_out[...])


def _ffn_kernel(*refs, layer, n_prompt_tiles, with_pe, final):
    n_mats = 5 if with_pe else 3
    if with_pe:
        hp_ref, hs_ref, pp_ref, ps_ref = refs[:4]
        rest = refs[4:]
    else:
        hp_ref, hs_ref = refs[:2]
        rest = refs[2:]
    chunks, rest = rest[:n_mats], rest[n_mats:]
    mats = rest[-n_mats:]
    if with_pe:
        f_norm, pe_norm, fin_norm, op_ref, os_ref = rest[:-n_mats]
        f_wg, f_wu, f_wd, pe_wg, pe_wp = mats
        pe_norm = _layer_view(pe_norm, layer)
    else:
        f_norm, op_ref, os_ref = rest[:-n_mats]
        f_wg, f_wu, f_wd = mats
    f_norm = _layer_view(f_norm, layer)
    step = pl.program_id(0)

    @pl.when(step < WEIGHT_STEPS)
    def _():
        for chunk_ref, dst_ref in zip(chunks, mats):
            _convert_chunk(step, chunk_ref, dst_ref)

    @pl.when(step >= WEIGHT_STEPS)
    def _():
        is_prompt = step < WEIGHT_STEPS + n_prompt_tiles
        h = _half_swiglu(jnp.where(is_prompt, hp_ref[...], hs_ref[...]), f_norm, f_wg, f_wu, f_wd)
        if with_pe:
            gate = jax.nn.sigmoid(_dot(_rms(h, pe_norm[...]).astype(BF16), pe_wg[...]))
            p = jnp.where(is_prompt, pp_ref[...], ps_ref[...])
            h = h + gate * _dot(p.astype(BF16), pe_wp[...])
        if final:
            h = _rms(h, fin_norm[...])

        @pl.when(is_prompt)
        def _():
            op_ref[...] = h

        @pl.when(jnp.logical_not(is_prompt))
        def _():
            os_ref[...] = h


def _resident(arr):
    del arr
    return pl.BlockSpec(memory_space=pltpu.VMEM)


def _mixer_call(x, hconv, hpool, mats, vecs, *, layer, hist_layer, nseg, seg_len, carry, start):
    assert len(mats) == N_MIXER_MATS and len(vecs) == N_MIXER_VECS
    B, S, _ = x.shape
    T = nseg * seg_len
    n_tiles = S // T
    n_streams = hconv.shape[1]

    def tile_of(i):
        j = jnp.maximum(i - WEIGHT_STEPS, 0)
        return j // n_tiles, lax.rem(j, n_tiles)

    def x_idx(i):
        b, t = tile_of(i)
        return b, t, 0

    def state_idx(i):
        b, t = tile_of(i)
        return (b if carry else t), 0, 0

    def hist_idx(i):
        return (hist_layer,) + state_idx(i)

    vn_rows = SGU_CHUNK if carry else seg_len
    in_specs = [
        pl.BlockSpec((None, T, D_MODEL), x_idx),
        pl.BlockSpec((None, nseg, CONV_HIST, WA), hist_idx),
        pl.BlockSpec((None, nseg, POOL_HIST, WC), hist_idx),
    ] + [_chunk_spec(w, layer) for w in mats] + [_resident(w) for w in vecs]
    out_shape = (
        jax.ShapeDtypeStruct((B, S, D_MODEL), F32),
        jax.ShapeDtypeStruct((n_streams, CONV_HIST, WA), F32),
        jax.ShapeDtypeStruct((n_streams, POOL_HIST, WC), F32),
        jax.ShapeDtypeStruct((n_streams, vn_rows, WB), F32),
    )
    out_specs = (
        pl.BlockSpec((None, T, D_MODEL), x_idx),
        pl.BlockSpec((nseg, CONV_HIST, WA), state_idx),
        pl.BlockSpec((nseg, POOL_HIST, WC), state_idx),
        pl.BlockSpec((None, vn_rows, WB) if carry else (nseg, vn_rows, WB), state_idx),
    )
    return pl.pallas_call(
        functools.partial(_mixer_kernel, layer=layer, nseg=nseg, seg_len=seg_len, n_tiles=n_tiles, carry=carry,
                          start=start),
        grid=(WEIGHT_STEPS + B * n_tiles,),
        in_specs=in_specs,
        out_specs=out_specs,
        out_shape=out_shape,
        scratch_shapes=[pltpu.VMEM(w.shape[1:], BF16) for w in mats] + [
            pltpu.VMEM((nseg, WA // LANES, CONV_PAD + seg_len, LANES), F32),
            pltpu.VMEM((nseg, WC // LANES, POOL_PAD + seg_len, LANES), F32),
        ],
        compiler_params=pltpu.CompilerParams(
            dimension_semantics=("arbitrary",),
            vmem_limit_bytes=V7X_VMEM_LIMIT_BYTES),
        name="mixer_carry" if carry else "mixer_streams",
    )(x, hconv, hpool, *mats, *vecs)


def _ffn_call(hp, hs, pe_inputs, big_w, wts, *, layer, final):
    n_p = hp.shape[0] // FFN_TILE
    n_s = hs.shape[0] // FFN_TILE
    prompt_idx = lambda i: (jnp.clip(i - WEIGHT_STEPS, 0, n_p - 1), 0)
    sample_idx = lambda i: (jnp.clip(i - WEIGHT_STEPS - n_p, 0, n_s - 1), 0)
    tok_specs = [pl.BlockSpec((FFN_TILE, D_MODEL), prompt_idx), pl.BlockSpec((FFN_TILE, D_MODEL), sample_idx)]
    pe_specs = [pl.BlockSpec((None, FFN_TILE, D_PLE), lambda i: (layer,) + prompt_idx(i)),
                pl.BlockSpec((None, FFN_TILE, D_PLE), lambda i: (layer,) + sample_idx(i))]
    with_pe = bool(pe_inputs)
    return pl.pallas_call(
        functools.partial(_ffn_kernel, layer=layer, n_prompt_tiles=n_p, with_pe=with_pe, final=final),
        grid=(WEIGHT_STEPS + n_p + n_s,),
        in_specs=(tok_specs + (pe_specs if with_pe else []) + [_chunk_spec(w, layer) for w in big_w]
                  + [_resident(w) for w in wts]),
        out_specs=tuple(tok_specs),
        out_shape=(jax.ShapeDtypeStruct(hp.shape, F32), jax.ShapeDtypeStruct(hs.shape, F32)),
        scratch_shapes=[pltpu.VMEM(w.shape[1:], BF16) for w in big_w],
        compiler_params=pltpu.CompilerParams(
            dimension_semantics=("arbitrary",),
            vmem_limit_bytes=V7X_VMEM_LIMIT_BYTES),
        name="ffn_pe" if with_pe else "ffn",
    )(hp, hs, *pe_inputs, *big_w, *wts)


def kernel(x_prompt, x_sample, p_prompt, p_sample, cache_conv, cache_pool, ffn1_norm, ffn1_w_gate, ffn1_w_up, ffn1_w_down, mix_norm, w_in, gate_b, conv_dw_w, conv_dw_b, conv_ln_g, conv_ln_b, conv_w_out, sgu_ln_g, sgu_ln_b, sgu_w_s, sgu_b_s, sgu_w_out, pool_w, pool_scale, pool_w_out, w_out, ffn2_norm, ffn2_w_gate, ffn2_w_up, ffn2_w_down, pe_norm, pe_w_gate, pe_w_proj, final_norm):
    depth = ffn1_norm.shape[0]
    bp, sp, _ = x_prompt.shape
    bs, ss, _ = x_sample.shape
    hp = x_prompt.reshape(bp * sp, D_MODEL)
    hs = x_sample.reshape(bs * ss, D_MODEL)
    pp_all = p_prompt.reshape(depth, bp * sp, D_PLE)
    ps_all = p_sample.reshape(depth, bs * ss, D_PLE)
    conv_zero = jnp.zeros((1, bp, CONV_HIST, WA), F32)
    pool_zero = jnp.zeros((1, bp, POOL_HIST, WC), F32)
    ffn1_mats = (ffn1_w_gate, ffn1_w_up, ffn1_w_down)
    ffn1_vecs = (ffn1_norm,)
    mixer_mats = (w_in, conv_w_out, sgu_w_out, pool_w_out, w_out)
    mixer_vecs = (
        mix_norm, gate_b,
        conv_dw_w, conv_dw_b, conv_ln_g, conv_ln_b,
        sgu_ln_g, sgu_ln_b, sgu_w_s, jnp.swapaxes(sgu_b_s, 1, 2),
        pool_w, pool_scale,
    )
    ffn2_mats = (ffn2_w_gate, ffn2_w_up, ffn2_w_down, pe_w_gate, pe_w_proj)
    ffn2_vecs = (ffn2_norm, pe_norm, final_norm.reshape(1, -1))
    conv_p, conv_s, pool_p, pool_s, v_p, v_s = [], [], [], [], [], []
    for i in range(depth):
        hp, hs = _ffn_call(hp, hs, (), ffn1_mats, ffn1_vecs, layer=i, final=False)
        hp, cp, pp, vp = _mixer_call(hp.reshape(bp, sp, D_MODEL), conv_zero, pool_zero, mixer_mats, mixer_vecs,
                                     layer=i, hist_layer=0, nseg=1, seg_len=PROMPT_TILE, carry=True, start=0)
        hs, cs, ps, vs = _mixer_call(hs.reshape(1, bs * ss, D_MODEL), cache_conv, cache_pool,
                                     mixer_mats, mixer_vecs, layer=i, hist_layer=i,
                                     nseg=SAMPLE_STREAMS_PER_TILE, seg_len=ss, carry=False, start=PAST_LEN)
        hp, hs = _ffn_call(hp.reshape(bp * sp, D_MODEL), hs.reshape(bs * ss, D_MODEL), (pp_all, ps_all),
                           ffn2_mats, ffn2_vecs, layer=i, final=i == depth - 1)

        conv_p.append(cp); conv_s.append(cs); pool_p.append(pp); pool_s.append(ps)
        v_p.append(vp); v_s.append(vs)
    return (hp.reshape(bp, sp, D_MODEL), hs.reshape(bs, ss, D_MODEL), jnp.stack(conv_p), jnp.stack(conv_s),
            jnp.stack(pool_p), jnp.stack(pool_s), jnp.stack(v_p), jnp.stack(v_s))
```

```python
import functools

import jax
import jax.numpy as jnp
from jax import lax
from jax.experimental import pallas as pl
from jax.experimental.pallas import tpu as pltpu

D_MODEL = 1024
D_PLE = 256
D_FF = 2816
WA = 512
CONV_W = 31
CONV_HIST = CONV_W - 1
WB = 512
SGU_GROUPS = 4
SGU_CHUNK = 128
SGU_GC = WB // SGU_GROUPS
WC = 512
POOL_WINDOWS = (2, 4, 8, 16)
POOL_GC = WC // len(POOL_WINDOWS)
POOL_HIST = max(POOL_WINDOWS) - 1
D_IN = 2 * WA + 2 * WB + WC + 3 * D_MODEL
PAST_LEN = 2048
EPS = 1e-6

F32 = jnp.float32
BF16 = jnp.bfloat16

V7X_VMEM_LIMIT_BYTES = 60 * 1024 * 1024
LANES = 128
CONV_PAD = 32
POOL_PAD = 16
ROW_BLOCK = 32
GATE_COLS = 512
FF_CHUNKS = ((0, 1024), (1024, 1024), (2048, 768))
WEIGHT_STEPS = 8
N_MIXER_MATS = 5
N_MIXER_VECS = 12

PROMPT_TILE = 256
SAMPLE_STREAMS_PER_TILE = 4
FFN_TILE = 512
FFN_SUB_TILE = 256


def _rms(x, g):
    return x * lax.rsqrt(jnp.mean(x * x, axis=-1, keepdims=True) + EPS) * g


def _ln(x, g, b):
    mu = jnp.mean(x, axis=-1, keepdims=True)
    xc = x - mu
    return xc * lax.rsqrt(jnp.mean(xc * xc, axis=-1, keepdims=True) + EPS) * g + b


def _dot(a, w):
    return jnp.dot(a, w, preferred_element_type=F32)


def _convert_chunk(step, chunk_ref, dst_ref):
    rows = chunk_ref.shape[0]
    r0 = pl.multiple_of(step * rows, rows)
    dst_ref[pl.ds(r0, rows), :] = chunk_ref[...].astype(BF16)


def _layer_view(ref, layer):
    return ref.at[pl.ds(layer, 1)] if len(ref.shape) == 2 else ref.at[layer]


def _chunk_spec(w, layer):
    rows = w.shape[1] // WEIGHT_STEPS
    return pl.BlockSpec((None, rows, w.shape[2]), lambda i: (layer, jnp.minimum(i, WEIGHT_STEPS - 1), 0))


def _half_swiglu(x, norm_ref, wg_ref, wu_ref, wd_ref):
    n = _rms(x, norm_ref[...]).astype(BF16)
    acc = None
    for c0, cw in FF_CHUNKS:
        g = _dot(n, wg_ref[:, c0:c0 + cw])
        u = _dot(n, wu_ref[:, c0:c0 + cw])
        d = _dot((g * jax.nn.sigmoid(g) * u).astype(BF16), wd_ref[c0:c0 + cw, :])
        acc = d if acc is None else acc + d
    return x + 0.5 * acc


def _mixer_kernel(*refs, layer, nseg, seg_len, n_tiles, carry, start):
    h_ref, hconv_ref, hpool_ref = refs[:3]
    rest = refs[3:]
    chunks, rest = rest[:N_MIXER_MATS], rest[N_MIXER_MATS:]
    vecs, rest = tuple(_layer_view(v, layer) for v in rest[:N_MIXER_VECS]), rest[N_MIXER_VECS:]
    outs, rest = rest[:4], rest[4:]
    mats, (cbuf, pbuf) = rest[:N_MIXER_MATS], rest[N_MIXER_MATS:]
    step = pl.program_id(0)

    @pl.when(step < WEIGHT_STEPS)
    def _():
        for chunk_ref, dst_ref in zip(chunks, mats):
            _convert_chunk(step, chunk_ref, dst_ref)

    @pl.when(step >= WEIGHT_STEPS)
    def _():
        tile = lax.rem(step - WEIGHT_STEPS, n_tiles)
        _mixer_tile(tile, h_ref, hconv_ref, hpool_ref, *mats, *vecs, *outs, cbuf, pbuf,
                    nseg=nseg, seg_len=seg_len, carry=carry, start=start)


def _mixer_tile(tile, h_ref, hconv_ref, hpool_ref,
                w_in, conv_wo, sgu_wo, pool_wo, w_out,
                mix_norm, gate_b, dw_w, dw_b, cln_g, cln_b, sln_g, sln_b, sgu_ws, sgu_bst, pool_w, pool_scale,
                h_out, conv_out, pool_out, vn_out, cbuf, pbuf, *, nseg, seg_len, carry, start):
    T = nseg * seg_len
    L = seg_len
    lane_groups = [slice(g * LANES, (g + 1) * LANES) for g in range(WA // LANES)]

    h = h_ref[...]
    n = _rms(h, mix_norm[...]).astype(BF16)

    def zin(c0, cw):
        return _dot(n, w_in[:, c0:c0 + cw])

    def load_history(s):
        for g, lanes in enumerate(lane_groups):
            cbuf[s, g, CONV_PAD - CONV_HIST:CONV_PAD, :] = hconv_ref[s, :, lanes]
            pbuf[s, g, POOL_PAD - POOL_HIST:POOL_PAD, :] = hpool_ref[s, :, lanes]

    if carry:
        pl.when(tile == 0)(functools.partial(load_history, 0))
    else:
        for s in range(nseg):
            load_history(s)

    glu = zin(0, WA) * jax.nn.sigmoid(zin(WA, WA))
    for s in range(nseg):
        for g, lanes in enumerate(lane_groups):
            cbuf[s, g, CONV_PAD:CONV_PAD + L, :] = glu[s * L:(s + 1) * L, lanes]

    def conv_block(s, r0):
        parts = []
        for g, lanes in enumerate(lane_groups):
            acc = None
            for k in range(CONV_W):
                off = CONV_PAD - CONV_HIST + k + r0
                term = dw_w[k:k + 1, lanes] * cbuf[s, g, off:off + ROW_BLOCK, :]
                acc = term if acc is None else acc + term
            parts.append(acc + dw_b[:, lanes])
        return jnp.concatenate(parts, axis=1)

    c_off = 2 * WA + 2 * WB
    g_off = c_off + WC
    gb = gate_b[...]

    def gate_part(j):
        cols = slice(j * GATE_COLS, (j + 1) * GATE_COLS)
        return jax.nn.sigmoid(zin(g_off + j * GATE_COLS, GATE_COLS) + gb[:, cols])

    mxu_tasks = ([functools.partial(zin, c_off, WC), functools.partial(zin, 2 * WA + WB, WB),
                  functools.partial(zin, 2 * WA, WB)]
                 + [functools.partial(gate_part, j) for j in range(3 * D_MODEL // GATE_COLS)])
    blocks = [(s, r0) for s in range(nseg) for r0 in range(0, L, ROW_BLOCK)]
    conv_rows, done = [], []
    for idx, (s, r0) in enumerate(blocks):
        conv_rows.append(conv_block(s, r0))
        while len(done) < (idx + 1) * len(mxu_tasks) // len(blocks):
            done.append(mxu_tasks[len(done)]())
    c, v, u = done[0], done[1], done[2]
    parts_per_gate = D_MODEL // GATE_COLS
    gates = [jnp.concatenate(done[3 + b * parts_per_gate:3 + (b + 1) * parts_per_gate], axis=1)
             for b in range(3)]

    for s in range(nseg):
        for g, lanes in enumerate(lane_groups):
            conv_out[s, :, lanes] = cbuf[s, g, L + CONV_PAD - CONV_HIST:L + CONV_PAD, :]
    if carry:
        for g in range(len(lane_groups)):
            cbuf[0, g, 0:CONV_PAD, :] = cbuf[0, g, L:L + CONV_PAD, :]

    ya = _ln(jnp.concatenate(conv_rows, axis=0), cln_g[...], cln_b[...])
    m = gates[0] * _dot((ya * jax.nn.sigmoid(ya)).astype(BF16), conv_wo[...])

    vn = _ln(v, sln_g[...], sln_b[...])
    if carry:
        vn_out[...] = vn[T - SGU_CHUNK:T]
    else:
        for s in range(nseg):
            vn_out[s] = vn[s * L:(s + 1) * L]
    rows = min(L, SGU_CHUNK)
    ri = lax.broadcasted_iota(jnp.int32, (rows, rows), 0)
    ci = lax.broadcasted_iota(jnp.int32, (rows, rows), 1)
    ws_tril = [jnp.where(ci <= ri, sgu_ws[g, 0:rows, 0:rows], 0.0) for g in range(SGU_GROUPS)]
    bst = sgu_bst[...]
    s_rows = []
    for r0 in range(0, T, rows):
        parts = []
        for g in range(SGU_GROUPS):
            sg = _dot(ws_tril[g], vn[r0:r0 + rows, g * SGU_GC:(g + 1) * SGU_GC])
            parts.append(sg + bst[0:rows, g:g + 1])
        s_rows.append(jnp.concatenate(parts, axis=1))
    m = m + gates[1] * _dot((u * jnp.concatenate(s_rows, axis=0)).astype(BF16), sgu_wo[...])

    for s in range(nseg):
        for g, lanes in enumerate(lane_groups):
            pbuf[s, g, POOL_PAD:POOL_PAD + L, :] = c[s * L:(s + 1) * L, lanes]
    pos0 = start + (tile * L if carry else 0)
    pos = pos0 + lax.broadcasted_iota(jnp.int32, (L, 1), 0)
    inv_cnt = [1.0 / jnp.minimum(pos + 1, w).astype(F32) for w in POOL_WINDOWS]
    d_rows = []
    for s in range(nseg):
        for r0 in range(0, L, ROW_BLOCK):
            parts = []
            for g, w in enumerate(POOL_WINDOWS):
                tot = None
                for j in range(w):
                    lo = POOL_PAD - j + r0
                    term = pbuf[s, g, lo:lo + ROW_BLOCK, :]
                    tot = term if tot is None else tot + term
                cur = pbuf[s, g, POOL_PAD + r0:POOL_PAD + r0 + ROW_BLOCK, :]
                parts.append(tot * inv_cnt[g][r0:r0 + ROW_BLOCK] - cur)
            d_rows.append(jnp.concatenate(parts, axis=1))
        for g, lanes in enumerate(lane_groups):
            pool_out[s, :, lanes] = pbuf[s, g, L + POOL_PAD - POOL_HIST:L + POOL_PAD, :]
    if carry:
        for g in range(len(lane_groups)):
            pbuf[0, g, 0:POOL_PAD, :] = pbuf[0, g, L:L + POOL_PAD, :]
    dd = jnp.concatenate(d_rows, axis=0)
    yc = jnp.concatenate(
        [_dot(dd[:, g * POOL_GC:(g + 1) * POOL_GC], pool_w[g]) for g in range(len(POOL_WINDOWS))], axis=1)
    m = m + gates[2] * _dot((yc * pool_scale[...]).astype(BF16), pool_wo[...])

    h_out[...] = h + _dot(m.astype(BF16), w_out[...])


def _ffn_kernel(*refs, layer, n_prompt_tiles, with_pe, final):
    n_mats = 5 if with_pe else 3
    if with_pe:
        hp_ref, hs_ref, pp_ref, ps_ref = refs[:4]
        rest = refs[4:]
    else:
        hp_ref, hs_ref = refs[:2]
        rest = refs[2:]
    chunks, rest = rest[:n_mats], rest[n_mats:]
    mats = rest[-n_mats:]
    if with_pe:
        f_norm, pe_norm, fin_norm, op_ref, os_ref = rest[:-n_mats]
        f_wg, f_wu, f_wd, pe_wg, pe_wp = mats
        pe_norm = _layer_view(pe_norm, layer)
    else:
        f_norm, op_ref, os_ref = rest[:-n_mats]
        f_wg, f_wu, f_wd = mats
    f_norm = _layer_view(f_norm, layer)
    step = pl.program_id(0)

    @pl.when(step < WEIGHT_STEPS)
    def _():
        for chunk_ref, dst_ref in zip(chunks, mats):
            _convert_chunk(step, chunk_ref, dst_ref)

    @pl.when(step >= WEIGHT_STEPS)
    def _():
        is_prompt = step < WEIGHT_STEPS + n_prompt_tiles
        sub_tiles = [slice(r0, r0 + FFN_SUB_TILE) for r0 in range(0, FFN_TILE, FFN_SUB_TILE)]
        results = []
        for rs in sub_tiles:
            h = _half_swiglu(jnp.where(is_prompt, hp_ref[rs, :], hs_ref[rs, :]), f_norm, f_wg, f_wu, f_wd)
            if with_pe:
                gate = jax.nn.sigmoid(_dot(_rms(h, pe_norm[...]).astype(BF16), pe_wg[...]))
                p = jnp.where(is_prompt, pp_ref[rs, :], ps_ref[rs, :])
                h = h + gate * _dot(p.astype(BF16), pe_wp[...])
            if final:
                h = _rms(h, fin_norm[...])
            results.append(h)

        @pl.when(is_prompt)
        def _():
            for rs, h in zip(sub_tiles, results):
                op_ref[rs, :] = h

        @pl.when(jnp.logical_not(is_prompt))
        def _():
            for rs, h in zip(sub_tiles, results):
                os_ref[rs, :] = h


def _resident(arr):
    del arr
    return pl.BlockSpec(memory_space=pltpu.VMEM)


def _mixer_call(x, hconv, hpool, mats, vecs, *, layer, hist_layer, nseg, seg_len, carry, start):
    assert len(mats) == N_MIXER_MATS and len(vecs) == N_MIXER_VECS
    B, S, _ = x.shape
    T = nseg * seg_len
    n_tiles = S // T
    n_streams = hconv.shape[1]

    def tile_of(i):
        j = jnp.maximum(i - WEIGHT_STEPS, 0)
        return j // n_tiles, lax.rem(j, n_tiles)

    def x_idx(i):
        b, t = tile_of(i)
        return b, t, 0

    def state_idx(i):
        b, t = tile_of(i)
        return (b if carry else t), 0, 0

    def hist_idx(i):
        return (hist_layer,) + state_idx(i)

    vn_rows = SGU_CHUNK if carry else seg_len
    in_specs = [
        pl.BlockSpec((None, T, D_MODEL), x_idx),
        pl.BlockSpec((None, nseg, CONV_HIST, WA), hist_idx),
        pl.BlockSpec((None, nseg, POOL_HIST, WC), hist_idx),
    ] + [_chunk_spec(w, layer) for w in mats] + [_resident(w) for w in vecs]
    out_shape = (
        jax.ShapeDtypeStruct((B, S, D_MODEL), F32),
        jax.ShapeDtypeStruct((n_streams, CONV_HIST, WA), F32),
        jax.ShapeDtypeStruct((n_streams, POOL_HIST, WC), F32),
        jax.ShapeDtypeStruct((n_streams, vn_rows, WB), F32),
    )
    out_specs = (
        pl.BlockSpec((None, T, D_MODEL), x_idx),
        pl.BlockSpec((nseg, CONV_HIST, WA), state_idx),
        pl.BlockSpec((nseg, POOL_HIST, WC), state_idx),
        pl.BlockSpec((None, vn_rows, WB) if carry else (nseg, vn_rows, WB), state_idx),
    )
    return pl.pallas_call(
        functools.partial(_mixer_kernel, layer=layer, nseg=nseg, seg_len=seg_len, n_tiles=n_tiles, carry=carry,
                          start=start),
        grid=(WEIGHT_STEPS + B * n_tiles,),
        in_specs=in_specs,
        out_specs=out_specs,
        out_shape=out_shape,
        scratch_shapes=[pltpu.VMEM(w.shape[1:], BF16) for w in mats] + [
            pltpu.VMEM((nseg, WA // LANES, CONV_PAD + seg_len, LANES), F32),
            pltpu.VMEM((nseg, WC // LANES, POOL_PAD + seg_len, LANES), F32),
        ],
        compiler_params=pltpu.CompilerParams(
            dimension_semantics=("arbitrary",),
            vmem_limit_bytes=V7X_VMEM_LIMIT_BYTES),
        name="mixer_carry" if carry else "mixer_streams",
    )(x, hconv, hpool, *mats, *vecs)


def _ffn_call(hp, hs, pe_inputs, big_w, wts, *, layer, final):
    n_p = hp.shape[0] // FFN_TILE
    n_s = hs.shape[0] // FFN_TILE
    prompt_idx = lambda i: (jnp.clip(i - WEIGHT_STEPS, 0, n_p - 1), 0)
    sample_idx = lambda i: (jnp.clip(i - WEIGHT_STEPS - n_p, 0, n_s - 1), 0)
    tok_specs = [pl.BlockSpec((FFN_TILE, D_MODEL), prompt_idx), pl.BlockSpec((FFN_TILE, D_MODEL), sample_idx)]
    pe_specs = [pl.BlockSpec((None, FFN_TILE, D_PLE), lambda i: (layer,) + prompt_idx(i)),
                pl.BlockSpec((None, FFN_TILE, D_PLE), lambda i: (layer,) + sample_idx(i))]
    with_pe = bool(pe_inputs)
    return pl.pallas_call(
        functools.partial(_ffn_kernel, layer=layer, n_prompt_tiles=n_p, with_pe=with_pe, final=final),
        grid=(WEIGHT_STEPS + n_p + n_s,),
        in_specs=(tok_specs + (pe_specs if with_pe else []) + [_chunk_spec(w, layer) for w in big_w]
                  + [_resident(w) for w in wts]),
        out_specs=tuple(tok_specs),
        out_shape=(jax.ShapeDtypeStruct(hp.shape, F32), jax.ShapeDtypeStruct(hs.shape, F32)),
        scratch_shapes=[pltpu.VMEM(w.shape[1:], BF16) for w in big_w],
        compiler_params=pltpu.CompilerParams(
            dimension_semantics=("arbitrary",),
            vmem_limit_bytes=V7X_VMEM_LIMIT_BYTES),
        name="ffn_pe" if with_pe else "ffn",
    )(hp, hs, *pe_inputs, *big_w, *wts)


def kernel(x_prompt, x_sample, p_prompt, p_sample, cache_conv, cache_pool, ffn1_norm, ffn1_w_gate, ffn1_w_up, ffn1_w_down, mix_norm, w_in, gate_b, conv_dw_w, conv_dw_b, conv_ln_g, conv_ln_b, conv_w_out, sgu_ln_g, sgu_ln_b, sgu_w_s, sgu_b_s, sgu_w_out, pool_w, pool_scale, pool_w_out, w_out, ffn2_norm, ffn2_w_gate, ffn2_w_up, ffn2_w_down, pe_norm, pe_w_gate, pe_w_proj, final_norm):
    depth = ffn1_norm.shape[0]
    bp, sp, _ = x_prompt.shape
    bs, ss, _ = x_sample.shape
    hp = x_prompt.reshape(bp * sp, D_MODEL)
    hs = x_sample.reshape(bs * ss, D_MODEL)
    pp_all = p_prompt.reshape(depth, bp * sp, D_PLE)
    ps_all = p_sample.reshape(depth, bs * ss, D_PLE)
    conv_zero = jnp.zeros((1, bp, CONV_HIST, WA), F32)
    pool_zero = jnp.zeros((1, bp, POOL_HIST, WC), F32)
    ffn1_mats = (ffn1_w_gate, ffn1_w_up, ffn1_w_down)
    ffn1_vecs = (ffn1_norm,)
    mixer_mats = (w_in, conv_w_out, sgu_w_out, pool_w_out, w_out)
    mixer_vecs = (
        mix_norm, gate_b,
        conv_dw_w, conv_dw_b, conv_ln_g, conv_ln_b,
        sgu_ln_g, sgu_ln_b, sgu_w_s, jnp.swapaxes(sgu_b_s, 1, 2),
        pool_w, pool_scale,
    )
    ffn2_mats = (ffn2_w_gate, ffn2_w_up, ffn2_w_down, pe_w_gate, pe_w_proj)
    ffn2_vecs = (ffn2_norm, pe_norm, final_norm.reshape(1, -1))
    conv_p, conv_s, pool_p, pool_s, v_p, v_s = [], [], [], [], [], []
    for i in range(depth):
        hp, hs = _ffn_call(hp, hs, (), ffn1_mats, ffn1_vecs, layer=i, final=False)
        hp, cp, pp, vp = _mixer_call(hp.reshape(bp, sp, D_MODEL), conv_zero, pool_zero, mixer_mats, mixer_vecs,
                                     layer=i, hist_layer=0, nseg=1, seg_len=PROMPT_TILE, carry=True, start=0)
        hs, cs, ps, vs = _mixer_call(hs.reshape(1, bs * ss, D_MODEL), cache_conv, cache_pool,
                                     mixer_mats, mixer_vecs, layer=i, hist_layer=i,
                                     nseg=SAMPLE_STREAMS_PER_TILE, seg_len=ss, carry=False, start=PAST_LEN)
        hp, hs = _ffn_call(hp.reshape(bp * sp, D_MODEL), hs.reshape(bs * ss, D_MODEL), (pp_all, ps_all),
                           ffn2_mats, ffn2_vecs, layer=i, final=i == depth - 1)

        conv_p.append(cp); conv_s.append(cs); pool_p.append(pp); pool_s.append(ps)
        v_p.append(vp); v_s.append(vs)
    return (hp.reshape(bp, sp, D_MODEL), hs.reshape(bs, ss, D_MODEL), jnp.stack(conv_p), jnp.stack(conv_s),
            jnp.stack(pool_p), jnp.stack(pool_s), jnp.stack(v_p), jnp.stack(v_s))
```

```python
import functools

import jax
import jax.numpy as jnp
from jax import lax
from jax.experimental import pallas as pl
from jax.experimental.pallas import tpu as pltpu

D_MODEL = 1024
D_PLE = 256
D_FF = 2816
WA = 512
CONV_W = 31
CONV_HIST = CONV_W - 1
WB = 512
SGU_GROUPS = 4
SGU_CHUNK = 128
SGU_GC = WB // SGU_GROUPS
WC = 512
POOL_WINDOWS = (2, 4, 8, 16)
POOL_GC = WC // len(POOL_WINDOWS)
POOL_HIST = max(POOL_WINDOWS) - 1
D_IN = 2 * WA + 2 * WB + WC + 3 * D_MODEL
PAST_LEN = 2048
EPS = 1e-6

F32 = jnp.float32
BF16 = jnp.bfloat16

V7X_VMEM_LIMIT_BYTES = 60 * 1024 * 1024
LANES = 128
CONV_PAD = 32
POOL_PAD = 16
ROW_BLOCK = 32
GATE_COLS = 512
FF_CHUNKS = ((0, 1024), (1024, 1024), (2048, 768))
WEIGHT_STEPS = 8
N_MIXER_MATS = 5
N_MIXER_VECS = 12

PROMPT_TILE = 256
MIXER_SUB_TILES = 2
SAMPLE_STREAMS_PER_TILE = 4
FFN_TILE = 512
FFN_SUB_TILE = 256


def _rms(x, g):
    return x * lax.rsqrt(jnp.mean(x * x, axis=-1, keepdims=True) + EPS) * g


def _ln(x, g, b):
    mu = jnp.mean(x, axis=-1, keepdims=True)
    xc = x - mu
    return xc * lax.rsqrt(jnp.mean(xc * xc, axis=-1, keepdims=True) + EPS) * g + b


def _dot(a, w):
    return jnp.dot(a, w, preferred_element_type=F32)


def _convert_chunk(step, chunk_ref, dst_ref):
    rows = chunk_ref.shape[0]
    r0 = pl.multiple_of(step * rows, rows)
    dst_ref[pl.ds(r0, rows), :] = chunk_ref[...].astype(BF16)


def _layer_view(ref, layer):
    return ref.at[pl.ds(layer, 1)] if len(ref.shape) == 2 else ref.at[layer]


def _chunk_spec(w, layer):
    rows = w.shape[1] // WEIGHT_STEPS
    return pl.BlockSpec((None, rows, w.shape[2]), lambda i: (layer, jnp.minimum(i, WEIGHT_STEPS - 1), 0))


def _half_swiglu(x, norm_ref, wg_ref, wu_ref, wd_ref):
    n = _rms(x, norm_ref[...]).astype(BF16)
    acc = None
    for c0, cw in FF_CHUNKS:
        g = _dot(n, wg_ref[:, c0:c0 + cw])
        u = _dot(n, wu_ref[:, c0:c0 + cw])
        d = _dot((g * jax.nn.sigmoid(g) * u).astype(BF16), wd_ref[c0:c0 + cw, :])
        acc = d if acc is None else acc + d
    return x + 0.5 * acc


def _mixer_kernel(*refs, layer, nseg, seg_len, n_tiles, carry, start):
    h_ref, hconv_ref, hpool_ref = refs[:3]
    rest = refs[3:]
    chunks, rest = rest[:N_MIXER_MATS], rest[N_MIXER_MATS:]
    vecs, rest = tuple(_layer_view(v, layer) for v in rest[:N_MIXER_VECS]), rest[N_MIXER_VECS:]
    outs, rest = rest[:4], rest[4:]
    mats, (cbuf, pbuf) = rest[:N_MIXER_MATS], rest[N_MIXER_MATS:]
    step = pl.program_id(0)

    @pl.when(step < WEIGHT_STEPS)
    def _():
        for chunk_ref, dst_ref in zip(chunks, mats):
            _convert_chunk(step, chunk_ref, dst_ref)

    @pl.when(step >= WEIGHT_STEPS)
    def _():
        tile = lax.rem(step - WEIGHT_STEPS, n_tiles)
        T = nseg * seg_len
        h_out, conv_out, pool_out, vn_out = outs
        for sub in range(MIXER_SUB_TILES):
            rows = pl.ds(sub * T, T)
            segs = pl.ds(0 if carry else sub * nseg, nseg)
            _mixer_tile(tile * MIXER_SUB_TILES + sub, h_ref.at[rows], hconv_ref.at[segs], hpool_ref.at[segs],
                        *mats, *vecs, h_out.at[rows], conv_out.at[segs], pool_out.at[segs],
                        vn_out if carry else vn_out.at[segs], cbuf.at[segs], pbuf.at[segs],
                        nseg=nseg, seg_len=seg_len, carry=carry, start=start)


def _mixer_tile(tile, h_ref, hconv_ref, hpool_ref,
                w_in, conv_wo, sgu_wo, pool_wo, w_out,
                mix_norm, gate_b, dw_w, dw_b, cln_g, cln_b, sln_g, sln_b, sgu_ws, sgu_bst, pool_w, pool_scale,
                h_out, conv_out, pool_out, vn_out, cbuf, pbuf, *, nseg, seg_len, carry, start):
    T = nseg * seg_len
    L = seg_len
    lane_groups = [slice(g * LANES, (g + 1) * LANES) for g in range(WA // LANES)]

    h = h_ref[...]
    n = _rms(h, mix_norm[...]).astype(BF16)

    def zin(c0, cw):
        return _dot(n, w_in[:, c0:c0 + cw])

    def load_history(s):
        for g, lanes in enumerate(lane_groups):
            cbuf[s, g, CONV_PAD - CONV_HIST:CONV_PAD, :] = hconv_ref[s, :, lanes]
            pbuf[s, g, POOL_PAD - POOL_HIST:POOL_PAD, :] = hpool_ref[s, :, lanes]

    if carry:
        pl.when(tile == 0)(functools.partial(load_history, 0))
    else:
        for s in range(nseg):
            load_history(s)

    glu = zin(0, WA) * jax.nn.sigmoid(zin(WA, WA))
    for s in range(nseg):
        for g, lanes in enumerate(lane_groups):
            cbuf[s, g, CONV_PAD:CONV_PAD + L, :] = glu[s * L:(s + 1) * L, lanes]

    def conv_block(s, r0):
        parts = []
        for g, lanes in enumerate(lane_groups):
            acc = None
            for k in range(CONV_W):
                off = CONV_PAD - CONV_HIST + k + r0
                term = dw_w[k:k + 1, lanes] * cbuf[s, g, off:off + ROW_BLOCK, :]
                acc = term if acc is None else acc + term
            parts.append(acc + dw_b[:, lanes])
        return jnp.concatenate(parts, axis=1)

    c_off = 2 * WA + 2 * WB
    g_off = c_off + WC
    gb = gate_b[...]

    def gate_part(j):
        cols = slice(j * GATE_COLS, (j + 1) * GATE_COLS)
        return jax.nn.sigmoid(zin(g_off + j * GATE_COLS, GATE_COLS) + gb[:, cols])

    mxu_tasks = ([functools.partial(zin, c_off, WC), functools.partial(zin, 2 * WA + WB, WB),
                  functools.partial(zin, 2 * WA, WB)]
                 + [functools.partial(gate_part, j) for j in range(3 * D_MODEL // GATE_COLS)])
    blocks = [(s, r0) for s in range(nseg) for r0 in range(0, L, ROW_BLOCK)]
    conv_rows, done = [], []
    for idx, (s, r0) in enumerate(blocks):
        conv_rows.append(conv_block(s, r0))
        while len(done) < (idx + 1) * len(mxu_tasks) // len(blocks):
            done.append(mxu_tasks[len(done)]())
    c, v, u = done[0], done[1], done[2]
    parts_per_gate = D_MODEL // GATE_COLS
    gates = [jnp.concatenate(done[3 + b * parts_per_gate:3 + (b + 1) * parts_per_gate], axis=1)
             for b in range(3)]

    for s in range(nseg):
        for g, lanes in enumerate(lane_groups):
            conv_out[s, :, lanes] = cbuf[s, g, L + CONV_PAD - CONV_HIST:L + CONV_PAD, :]
    if carry:
        for g in range(len(lane_groups)):
            cbuf[0, g, 0:CONV_PAD, :] = cbuf[0, g, L:L + CONV_PAD, :]

    ya = _ln(jnp.concatenate(conv_rows, axis=0), cln_g[...], cln_b[...])
    m = gates[0] * _dot((ya * jax.nn.sigmoid(ya)).astype(BF16), conv_wo[...])

    vn = _ln(v, sln_g[...], sln_b[...])
    if carry:
        vn_out[...] = vn[T - SGU_CHUNK:T]
    else:
        for s in range(nseg):
            vn_out[s] = vn[s * L:(s + 1) * L]
    rows = min(L, SGU_CHUNK)
    ri = lax.broadcasted_iota(jnp.int32, (rows, rows), 0)
    ci = lax.broadcasted_iota(jnp.int32, (rows, rows), 1)
    ws_tril = [jnp.where(ci <= ri, sgu_ws[g, 0:rows, 0:rows], 0.0) for g in range(SGU_GROUPS)]
    bst = sgu_bst[...]
    s_rows = []
    for r0 in range(0, T, rows):
        parts = []
        for g in range(SGU_GROUPS):
            sg = _dot(ws_tril[g], vn[r0:r0 + rows, g * SGU_GC:(g + 1) * SGU_GC])
            parts.append(sg + bst[0:rows, g:g + 1])
        s_rows.append(jnp.concatenate(parts, axis=1))
    m = m + gates[1] * _dot((u * jnp.concatenate(s_rows, axis=0)).astype(BF16), sgu_wo[...])

    for s in range(nseg):
        for g, lanes in enumerate(lane_groups):
            pbuf[s, g, POOL_PAD:POOL_PAD + L, :] = c[s * L:(s + 1) * L, lanes]
    pos0 = start + (tile * L if carry else 0)
    pos = pos0 + lax.broadcasted_iota(jnp.int32, (L, 1), 0)
    inv_cnt = [1.0 / jnp.minimum(pos + 1, w).astype(F32) for w in POOL_WINDOWS]
    d_rows = []
    for s in range(nseg):
        for r0 in range(0, L, ROW_BLOCK):
            parts = []
            for g, w in enumerate(POOL_WINDOWS):
                tot = None
                for j in range(w):
                    lo = POOL_PAD - j + r0
                    term = pbuf[s, g, lo:lo + ROW_BLOCK, :]
                    tot = term if tot is None else tot + term
                cur = pbuf[s, g, POOL_PAD + r0:POOL_PAD + r0 + ROW_BLOCK, :]
                parts.append(tot * inv_cnt[g][r0:r0 + ROW_BLOCK] - cur)
            d_rows.append(jnp.concatenate(parts, axis=1))
        for g, lanes in enumerate(lane_groups):
            pool_out[s, :, lanes] = pbuf[s, g, L + POOL_PAD - POOL_HIST:L + POOL_PAD, :]
    if carry:
        for g in range(len(lane_groups)):
            pbuf[0, g, 0:POOL_PAD, :] = pbuf[0, g, L:L + POOL_PAD, :]
    dd = jnp.concatenate(d_rows, axis=0)
    yc = jnp.concatenate(
        [_dot(dd[:, g * POOL_GC:(g + 1) * POOL_GC], pool_w[g]) for g in range(len(POOL_WINDOWS))], axis=1)
    m = m + gates[2] * _dot((yc * pool_scale[...]).astype(BF16), pool_wo[...])

    h_out[...] = h + _dot(m.astype(BF16), w_out[...])


def _ffn_kernel(*refs, layer, n_prompt_tiles, with_pe, final):
    n_mats = 5 if with_pe else 3
    if with_pe:
        hp_ref, hs_ref, pp_ref, ps_ref = refs[:4]
        rest = refs[4:]
    else:
        hp_ref, hs_ref = refs[:2]
        rest = refs[2:]
    chunks, rest = rest[:n_mats], rest[n_mats:]
    mats = rest[-n_mats:]
    if with_pe:
        f_norm, pe_norm, fin_norm, op_ref, os_ref = rest[:-n_mats]
        f_wg, f_wu, f_wd, pe_wg, pe_wp = mats
        pe_norm = _layer_view(pe_norm, layer)
    else:
        f_norm, op_ref, os_ref = rest[:-n_mats]
        f_wg, f_wu, f_wd = mats
    f_norm = _layer_view(f_norm, layer)
    step = pl.program_id(0)

    @pl.when(step < WEIGHT_STEPS)
    def _():
        for chunk_ref, dst_ref in zip(chunks, mats):
            _convert_chunk(step, chunk_ref, dst_ref)

    @pl.when(step >= WEIGHT_STEPS)
    def _():
        is_prompt = step < WEIGHT_STEPS + n_prompt_tiles
        sub_tiles = [slice(r0, r0 + FFN_SUB_TILE) for r0 in range(0, FFN_TILE, FFN_SUB_TILE)]
        results = []
        for rs in sub_tiles:
            h = _half_swiglu(jnp.where(is_prompt, hp_ref[rs, :], hs_ref[rs, :]), f_norm, f_wg, f_wu, f_wd)
            if with_pe:
                gate = jax.nn.sigmoid(_dot(_rms(h, pe_norm[...]).astype(BF16), pe_wg[...]))
                p = jnp.where(is_prompt, pp_ref[rs, :], ps_ref[rs, :])
                h = h + gate * _dot(p.astype(BF16), pe_wp[...])
            if final:
                h = _rms(h, fin_norm[...])
            results.append(h)

        @pl.when(is_prompt)
        def _():
            for rs, h in zip(sub_tiles, results):
                op_ref[rs, :] = h

        @pl.when(jnp.logical_not(is_prompt))
        def _():
            for rs, h in zip(sub_tiles, results):
                os_ref[rs, :] = h


def _resident(arr):
    del arr
    return pl.BlockSpec(memory_space=pltpu.VMEM)


def _mixer_call(x, hconv, hpool, mats, vecs, *, layer, hist_layer, nseg, seg_len, carry, start):
    assert len(mats) == N_MIXER_MATS and len(vecs) == N_MIXER_VECS
    B, S, _ = x.shape
    T = MIXER_SUB_TILES * nseg * seg_len
    n_tiles = S // T
    n_streams = hconv.shape[1]
    hseg = nseg if carry else MIXER_SUB_TILES * nseg

    def tile_of(i):
        j = jnp.maximum(i - WEIGHT_STEPS, 0)
        return j // n_tiles, lax.rem(j, n_tiles)

    def x_idx(i):
        b, t = tile_of(i)
        return b, t, 0

    def state_idx(i):
        b, t = tile_of(i)
        return (b if carry else t), 0, 0

    def hist_idx(i):
        return (hist_layer,) + state_idx(i)

    vn_rows = SGU_CHUNK if carry else seg_len
    in_specs = [
        pl.BlockSpec((None, T, D_MODEL), x_idx),
        pl.BlockSpec((None, hseg, CONV_HIST, WA), hist_idx),
        pl.BlockSpec((None, hseg, POOL_HIST, WC), hist_idx),
    ] + [_chunk_spec(w, layer) for w in mats] + [_resident(w) for w in vecs]
    out_shape = (
        jax.ShapeDtypeStruct((B, S, D_MODEL), F32),
        jax.ShapeDtypeStruct((n_streams, CONV_HIST, WA), F32),
        jax.ShapeDtypeStruct((n_streams, POOL_HIST, WC), F32),
        jax.ShapeDtypeStruct((n_streams, vn_rows, WB), F32),
    )
    out_specs = (
        pl.BlockSpec((None, T, D_MODEL), x_idx),
        pl.BlockSpec((hseg, CONV_HIST, WA), state_idx),
        pl.BlockSpec((hseg, POOL_HIST, WC), state_idx),
        pl.BlockSpec((None, vn_rows, WB) if carry else (hseg, vn_rows, WB), state_idx),
    )
    return pl.pallas_call(
        functools.partial(_mixer_kernel, layer=layer, nseg=nseg, seg_len=seg_len, n_tiles=n_tiles, carry=carry,
                          start=start),
        grid=(WEIGHT_STEPS + B * n_tiles,),
        in_specs=in_specs,
        out_specs=out_specs,
        out_shape=out_shape,
        scratch_shapes=[pltpu.VMEM(w.shape[1:], BF16) for w in mats] + [
            pltpu.VMEM((hseg, WA // LANES, CONV_PAD + seg_len, LANES), F32),
            pltpu.VMEM((hseg, WC // LANES, POOL_PAD + seg_len, LANES), F32),
        ],
        compiler_params=pltpu.CompilerParams(
            dimension_semantics=("arbitrary",),
            vmem_limit_bytes=V7X_VMEM_LIMIT_BYTES),
        name="mixer_carry" if carry else "mixer_streams",
    )(x, hconv, hpool, *mats, *vecs)


def _ffn_call(hp, hs, pe_inputs, big_w, wts, *, layer, final):
    n_p = hp.shape[0] // FFN_TILE
    n_s = hs.shape[0] // FFN_TILE
    prompt_idx = lambda i: (jnp.clip(i - WEIGHT_STEPS, 0, n_p - 1), 0)
    sample_idx = lambda i: (jnp.clip(i - WEIGHT_STEPS - n_p, 0, n_s - 1), 0)
    tok_specs = [pl.BlockSpec((FFN_TILE, D_MODEL), prompt_idx), pl.BlockSpec((FFN_TILE, D_MODEL), sample_idx)]
    pe_specs = [pl.BlockSpec((None, FFN_TILE, D_PLE), lambda i: (layer,) + prompt_idx(i)),
                pl.BlockSpec((None, FFN_TILE, D_PLE), lambda i: (layer,) + sample_idx(i))]
    with_pe = bool(pe_inputs)
    return pl.pallas_call(
        functools.partial(_ffn_kernel, layer=layer, n_prompt_tiles=n_p, with_pe=with_pe, final=final),
        grid=(WEIGHT_STEPS + n_p + n_s,),
        in_specs=(tok_specs + (pe_specs if with_pe else []) + [_chunk_spec(w, layer) for w in big_w]
                  + [_resident(w) for w in wts]),
        out_specs=tuple(tok_specs),
        out_shape=(jax.ShapeDtypeStruct(hp.shape, F32), jax.ShapeDtypeStruct(hs.shape, F32)),
        scratch_shapes=[pltpu.VMEM(w.shape[1:], BF16) for w in big_w],
        compiler_params=pltpu.CompilerParams(
            dimension_semantics=("arbitrary",),
            vmem_limit_bytes=V7X_VMEM_LIMIT_BYTES),
        name="ffn_pe" if with_pe else "ffn",
    )(hp, hs, *pe_inputs, *big_w, *wts)


def kernel(x_prompt, x_sample, p_prompt, p_sample, cache_conv, cache_pool, ffn1_norm, ffn1_w_gate, ffn1_w_up, ffn1_w_down, mix_norm, w_in, gate_b, conv_dw_w, conv_dw_b, conv_ln_g, conv_ln_b, conv_w_out, sgu_ln_g, sgu_ln_b, sgu_w_s, sgu_b_s, sgu_w_out, pool_w, pool_scale, pool_w_out, w_out, ffn2_norm, ffn2_w_gate, ffn2_w_up, ffn2_w_down, pe_norm, pe_w_gate, pe_w_proj, final_norm):
    depth = ffn1_norm.shape[0]
    bp, sp, _ = x_prompt.shape
    bs, ss, _ = x_sample.shape
    hp = x_prompt.reshape(bp * sp, D_MODEL)
    hs = x_sample.reshape(bs * ss, D_MODEL)
    pp_all = p_prompt.reshape(depth, bp * sp, D_PLE)
    ps_all = p_sample.reshape(depth, bs * ss, D_PLE)
    conv_zero = jnp.zeros((1, bp, CONV_HIST, WA), F32)
    pool_zero = jnp.zeros((1, bp, POOL_HIST, WC), F32)
    ffn1_mats = (ffn1_w_gate, ffn1_w_up, ffn1_w_down)
    ffn1_vecs = (ffn1_norm,)
    mixer_mats = (w_in, conv_w_out, sgu_w_out, pool_w_out, w_out)
    mixer_vecs = (
        mix_norm, gate_b,
        conv_dw_w, conv_dw_b, conv_ln_g, conv_ln_b,
        sgu_ln_g, sgu_ln_b, sgu_w_s, jnp.swapaxes(sgu_b_s, 1, 2),
        pool_w, pool_scale,
    )
    ffn2_mats = (ffn2_w_gate, ffn2_w_up, ffn2_w_down, pe_w_gate, pe_w_proj)
    ffn2_vecs = (ffn2_norm, pe_norm, final_norm.reshape(1, -1))
    conv_p, conv_s, pool_p, pool_s, v_p, v_s = [], [], [], [], [], []
    for i in range(depth):
        hp, hs = _ffn_call(hp, hs, (), ffn1_mats, ffn1_vecs, layer=i, final=False)
        hp, cp, pp, vp = _mixer_call(hp.reshape(bp, sp, D_MODEL), conv_zero, pool_zero, mixer_mats, mixer_vecs,
                                     layer=i, hist_layer=0, nseg=1, seg_len=PROMPT_TILE, carry=True, start=0)
        hs, cs, ps, vs = _mixer_call(hs.reshape(1, bs * ss, D_MODEL), cache_conv, cache_pool,
                                     mixer_mats, mixer_vecs, layer=i, hist_layer=i,
                                     nseg=SAMPLE_STREAMS_PER_TILE, seg_len=ss, carry=False, start=PAST_LEN)
        hp, hs = _ffn_call(hp.reshape(bp * sp, D_MODEL), hs.reshape(bs * ss, D_MODEL), (pp_all, ps_all),
                           ffn2_mats, ffn2_vecs, layer=i, final=i == depth - 1)

        conv_p.append(cp); conv_s.append(cs); pool_p.append(pp); pool_s.append(ps)
        v_p.append(vp); v_s.append(vs)
    return (hp.reshape(bp, sp, D_MODEL), hs.reshape(bs, ss, D_MODEL), jnp.stack(conv_p), jnp.stack(conv_s),
            jnp.stack(pool_p), jnp.stack(pool_s), jnp.stack(v_p), jnp.stack(v_s))
```

```python
import functools

import jax
import jax.numpy as jnp
from jax import lax
from jax.experimental import pallas as pl
from jax.experimental.pallas import tpu as pltpu

D_MODEL = 1024
D_PLE = 256
D_FF = 2816
WA = 512
CONV_W = 31
CONV_HIST = CONV_W - 1
WB = 512
SGU_GROUPS = 4
SGU_CHUNK = 128
SGU_GC = WB // SGU_GROUPS
WC = 512
POOL_WINDOWS = (2, 4, 8, 16)
POOL_GC = WC // len(POOL_WINDOWS)
POOL_HIST = max(POOL_WINDOWS) - 1
D_IN = 2 * WA + 2 * WB + WC + 3 * D_MODEL
PAST_LEN = 2048
EPS = 1e-6

F32 = jnp.float32
BF16 = jnp.bfloat16

V7X_VMEM_LIMIT_BYTES = 60 * 1024 * 1024
LANES = 128
CONV_PAD = 32
POOL_PAD = 16
ROW_BLOCK = 32
GATE_COLS = 512
FF_CHUNKS = ((0, 1024), (1024, 1024), (2048, 768))
WEIGHT_STEPS = 8
N_MIXER_MATS = 5
N_MIXER_VECS = 12

PROMPT_TILE = 256
MIXER_SUB_TILES = 2
SAMPLE_STREAMS_PER_TILE = 4
FFN_TILE = 512
FFN_SUB_TILE = 256


def _rms(x, g):
    return x * lax.rsqrt(jnp.mean(x * x, axis=-1, keepdims=True) + EPS) * g


def _ln(x, g, b):
    mu = jnp.mean(x, axis=-1, keepdims=True)
    xc = x - mu
    return xc * lax.rsqrt(jnp.mean(xc * xc, axis=-1, keepdims=True) + EPS) * g + b


def _dot(a, w):
    return jnp.dot(a, w, preferred_element_type=F32)


def _convert_chunk(step, chunk_ref, dst_ref):
    rows = chunk_ref.shape[0]
    r0 = pl.multiple_of(step * rows, rows)
    dst_ref[pl.ds(r0, rows), :] = chunk_ref[...].astype(BF16)


def _layer_view(ref, layer):
    return ref.at[pl.ds(layer, 1)] if len(ref.shape) == 2 else ref.at[layer]


def _chunk_spec(w, layer):
    rows = w.shape[1] // WEIGHT_STEPS
    return pl.BlockSpec((None, rows, w.shape[2]), lambda i: (layer, jnp.minimum(i, WEIGHT_STEPS - 1), 0))


def _half_swiglu(x, norm_ref, wg_ref, wu_ref, wd_ref):
    n = _rms(x, norm_ref[...]).astype(BF16)
    acc = None
    for c0, cw in FF_CHUNKS:
        g = _dot(n, wg_ref[:, c0:c0 + cw])
        u = _dot(n, wu_ref[:, c0:c0 + cw])
        d = _dot((g * jax.nn.sigmoid(g) * u).astype(BF16), wd_ref[c0:c0 + cw, :])
        acc = d if acc is None else acc + d
    return x + 0.5 * acc


def _mixer_kernel(*refs, layer, nseg, seg_len, n_tiles, carry, start):
    h_ref, hconv_ref, hpool_ref = refs[:3]
    rest = refs[3:]
    chunks, rest = rest[:N_MIXER_MATS], rest[N_MIXER_MATS:]
    vecs, rest = tuple(_layer_view(v, layer) for v in rest[:N_MIXER_VECS]), rest[N_MIXER_VECS:]
    outs, rest = rest[:4], rest[4:]
    mats, (cbuf, pbuf) = rest[:N_MIXER_MATS], rest[N_MIXER_MATS:]
    step = pl.program_id(0)

    @pl.when(step < WEIGHT_STEPS)
    def _():
        for chunk_ref, dst_ref in zip(chunks, mats):
            _convert_chunk(step, chunk_ref, dst_ref)

    @pl.when(step >= WEIGHT_STEPS)
    def _():
        tile = lax.rem(step - WEIGHT_STEPS, n_tiles)
        T = nseg * seg_len
        h_out, conv_out, pool_out, vn_out = outs
        for sub in range(MIXER_SUB_TILES):
            rows = pl.ds(sub * T, T)
            segs = pl.ds(0 if carry else sub * nseg, nseg)
            _mixer_tile(tile * MIXER_SUB_TILES + sub, h_ref.at[rows], hconv_ref.at[segs], hpool_ref.at[segs],
                        *mats, *vecs, h_out.at[rows], conv_out.at[segs], pool_out.at[segs],
                        vn_out if carry else vn_out.at[segs], cbuf.at[segs], pbuf.at[segs],
                        nseg=nseg, seg_len=seg_len, carry=carry, start=start)


def _mixer_tile(tile, h_ref, hconv_ref, hpool_ref,
                w_in, conv_wo, sgu_wo, pool_wo, w_out,
                mix_norm, gate_b, dw_w, dw_b, cln_g, cln_b, sln_g, sln_b, sgu_ws, sgu_bst, pool_w, pool_scale,
                h_out, conv_out, pool_out, vn_out, cbuf, pbuf, *, nseg, seg_len, carry, start):
    T = nseg * seg_len
    L = seg_len
    lane_groups = [slice(g * LANES, (g + 1) * LANES) for g in range(WA // LANES)]

    h = h_ref[...]
    n = _rms(h, mix_norm[...]).astype(BF16)

    def zin(c0, cw):
        return _dot(n, w_in[:, c0:c0 + cw])

    def load_history(s):
        for g, lanes in enumerate(lane_groups):
            cbuf[s, g, CONV_PAD - CONV_HIST:CONV_PAD, :] = hconv_ref[s, :, lanes]
            pbuf[s, g, POOL_PAD - POOL_HIST:POOL_PAD, :] = hpool_ref[s, :, lanes]

    if carry:
        pl.when(tile == 0)(functools.partial(load_history, 0))
    else:
        for s in range(nseg):
            load_history(s)

    glu = zin(0, WA) * jax.nn.sigmoid(zin(WA, WA))
    for s in range(nseg):
        for g, lanes in enumerate(lane_groups):
            cbuf[s, g, CONV_PAD:CONV_PAD + L, :] = glu[s * L:(s + 1) * L, lanes]

    def conv_block(s, r0):
        parts = []
        for g, lanes in enumerate(lane_groups):
            acc = None
            for k in range(CONV_W):
                off = CONV_PAD - CONV_HIST + k + r0
                term = dw_w[k:k + 1, lanes] * cbuf[s, g, off:off + ROW_BLOCK, :]
                acc = term if acc is None else acc + term
            parts.append(acc + dw_b[:, lanes])
        return jnp.concatenate(parts, axis=1)

    c_off = 2 * WA + 2 * WB
    g_off = c_off + WC
    gb = gate_b[...]

    def gate_part(j):
        cols = slice(j * GATE_COLS, (j + 1) * GATE_COLS)
        return jax.nn.sigmoid(zin(g_off + j * GATE_COLS, GATE_COLS) + gb[:, cols])

    mxu_tasks = ([functools.partial(zin, c_off, WC), functools.partial(zin, 2 * WA + WB, WB),
                  functools.partial(zin, 2 * WA, WB)]
                 + [functools.partial(gate_part, j) for j in range(3 * D_MODEL // GATE_COLS)])
    blocks = [(s, r0) for s in range(nseg) for r0 in range(0, L, ROW_BLOCK)]
    conv_rows, done = [], []
    for idx, (s, r0) in enumerate(blocks):
        conv_rows.append(conv_block(s, r0))
        while len(done) < (idx + 1) * len(mxu_tasks) // len(blocks):
            done.append(mxu_tasks[len(done)]())
    c, v, u = done[0], done[1], done[2]
    parts_per_gate = D_MODEL // GATE_COLS
    gates = [jnp.concatenate(done[3 + b * parts_per_gate:3 + (b + 1) * parts_per_gate], axis=1)
             for b in range(3)]

    for s in range(nseg):
        for g, lanes in enumerate(lane_groups):
            conv_out[s, :, lanes] = cbuf[s, g, L + CONV_PAD - CONV_HIST:L + CONV_PAD, :]
    if carry:
        for g in range(len(lane_groups)):
            cbuf[0, g, 0:CONV_PAD, :] = cbuf[0, g, L:L + CONV_PAD, :]

    ya = _ln(jnp.concatenate(conv_rows, axis=0), cln_g[...], cln_b[...])
    m = gates[0] * _dot((ya * jax.nn.sigmoid(ya)).astype(BF16), conv_wo[...])

    vn = _ln(v, sln_g[...], sln_b[...])
    if carry:
        vn_out[...] = vn[T - SGU_CHUNK:T]
    else:
        for s in range(nseg):
            vn_out[s] = vn[s * L:(s + 1) * L]
    rows = min(L, SGU_CHUNK)
    ri = lax.broadcasted_iota(jnp.int32, (rows, rows), 0)
    ci = lax.broadcasted_iota(jnp.int32, (rows, rows), 1)
    ws_tril = [jnp.where(ci <= ri, sgu_ws[g, 0:rows, 0:rows], 0.0) for g in range(SGU_GROUPS)]
    bst = sgu_bst[...]
    s_rows = []
    for r0 in range(0, T, rows):
        parts = []
        for g in range(SGU_GROUPS):
            sg = _dot(ws_tril[g], vn[r0:r0 + rows, g * SGU_GC:(g + 1) * SGU_GC])
            parts.append(sg + bst[0:rows, g:g + 1])
        s_rows.append(jnp.concatenate(parts, axis=1))
    m = m + gates[1] * _dot((u * jnp.concatenate(s_rows, axis=0)).astype(BF16), sgu_wo[...])

    for s in range(nseg):
        for g, lanes in enumerate(lane_groups):
            pbuf[s, g, POOL_PAD:POOL_PAD + L, :] = c[s * L:(s + 1) * L, lanes]
    pos0 = start + (tile * L if carry else 0)
    pos = pos0 + lax.broadcasted_iota(jnp.int32, (L, 1), 0)
    inv_cnt = [1.0 / jnp.minimum(pos + 1, w).astype(F32) for w in POOL_WINDOWS]
    d_rows = []
    for s in range(nseg):
        for r0 in range(0, L, ROW_BLOCK):
            parts = []
            for g, w in enumerate(POOL_WINDOWS):
                tot = None
                for j in range(w):
                    lo = POOL_PAD - j + r0
                    term = pbuf[s, g, lo:lo + ROW_BLOCK, :]
                    tot = term if tot is None else tot + term
                cur = pbuf[s, g, POOL_PAD + r0:POOL_PAD + r0 + ROW_BLOCK, :]
                parts.append(tot * inv_cnt[g][r0:r0 + ROW_BLOCK] - cur)
            d_rows.append(jnp.concatenate(parts, axis=1))
        for g, lanes in enumerate(lane_groups):
            pool_out[s, :, lanes] = pbuf[s, g, L + POOL_PAD - POOL_HIST:L + POOL_PAD, :]
    if carry:
        for g in range(len(lane_groups)):
            pbuf[0, g, 0:POOL_PAD, :] = pbuf[0, g, L:L + POOL_PAD, :]
    dd = jnp.concatenate(d_rows, axis=0)
    yc = jnp.concatenate(
        [_dot(dd[:, g * POOL_GC:(g + 1) * POOL_GC], pool_w[g]) for g in range(len(POOL_WINDOWS))], axis=1)
    m = m + gates[2] * _dot((yc * pool_scale[...]).astype(BF16), pool_wo[...])

    h_out[...] = h + _dot(m.astype(BF16), w_out[...])


def _ffn_kernel(*refs, layer, n_prompt_tiles, split_in, split_out, with_pe, final):
    n_mats = 5 if with_pe else 3
    n_in = 2 if split_in else 1
    n_out = 2 if split_out else 1
    tok_in, rest = refs[:n_in], refs[n_in:]
    if with_pe:
        (pp_ref, ps_ref), rest = rest[:2], rest[2:]
    chunks, rest = rest[:n_mats], rest[n_mats:]
    mats = rest[-n_mats:]
    vecs, tok_out = rest[:-n_mats - n_out], rest[-n_mats - n_out:-n_mats]
    if with_pe:
        f_norm, pe_norm, fin_norm = vecs
        f_wg, f_wu, f_wd, pe_wg, pe_wp = mats
        pe_norm = _layer_view(pe_norm, layer)
    else:
        f_norm, = vecs
        f_wg, f_wu, f_wd = mats
    f_norm = _layer_view(f_norm, layer)
    step = pl.program_id(0)

    @pl.when(step < WEIGHT_STEPS)
    def _():
        for chunk_ref, dst_ref in zip(chunks, mats):
            _convert_chunk(step, chunk_ref, dst_ref)

    @pl.when(step >= WEIGHT_STEPS)
    def _():
        is_prompt = step < WEIGHT_STEPS + n_prompt_tiles

        def pick(prompt_ref, sample_ref, rs):
            return jnp.where(is_prompt, prompt_ref[rs, :], sample_ref[rs, :])

        sub_tiles = [slice(r0, r0 + FFN_SUB_TILE) for r0 in range(0, FFN_TILE, FFN_SUB_TILE)]
        results = [_half_swiglu(pick(*tok_in, rs) if split_in else tok_in[0][rs, :], f_norm, f_wg, f_wu, f_wd)
                   for rs in sub_tiles]
        if with_pe:
            gated = []
            for rs, h in zip(sub_tiles, results):
                gate = jax.nn.sigmoid(_dot(_rms(h, pe_norm[...]).astype(BF16), pe_wg[...]))
                gated.append(h + gate * _dot(pick(pp_ref, ps_ref, rs).astype(BF16), pe_wp[...]))
            results = gated
        if final:
            results = [_rms(h, fin_norm[...]) for h in results]

        def store(o_ref):
            for rs, h in zip(sub_tiles, results):
                o_ref[rs, :] = h

        if split_out:
            pl.when(is_prompt)(functools.partial(store, tok_out[0]))
            pl.when(jnp.logical_not(is_prompt))(functools.partial(store, tok_out[1]))
        else:
            store(tok_out[0])


def _resident(arr):
    del arr
    return pl.BlockSpec(memory_space=pltpu.VMEM)


def _mixer_call(x, row0, n_rows, stream_len, hconv, hpool, mats, vecs, *, layer, hist_layer, nseg, seg_len,
                carry, start):
    assert len(mats) == N_MIXER_MATS and len(vecs) == N_MIXER_VECS
    T = MIXER_SUB_TILES * nseg * seg_len
    n_tiles = stream_len // T if carry else n_rows // T
    total_tiles = n_rows // T
    tile0 = row0 // T
    n_streams = hconv.shape[1]
    hseg = nseg if carry else MIXER_SUB_TILES * nseg

    def tile_of(i):
        j = jnp.maximum(i - WEIGHT_STEPS, 0)
        return j // n_tiles, lax.rem(j, n_tiles)

    def x_idx(i):
        return tile0 + jnp.maximum(i - WEIGHT_STEPS, 0), 0

    def out_idx(i):
        return jnp.maximum(i - WEIGHT_STEPS, 0), 0

    def state_idx(i):
        b, t = tile_of(i)
        return (b if carry else t), 0, 0

    def hist_idx(i):
        return (hist_layer,) + state_idx(i)

    vn_rows = SGU_CHUNK if carry else seg_len
    in_specs = [
        pl.BlockSpec((T, D_MODEL), x_idx),
        pl.BlockSpec((None, hseg, CONV_HIST, WA), hist_idx),
        pl.BlockSpec((None, hseg, POOL_HIST, WC), hist_idx),
    ] + [_chunk_spec(w, layer) for w in mats] + [_resident(w) for w in vecs]
    out_shape = (
        jax.ShapeDtypeStruct((n_rows, D_MODEL), F32),
        jax.ShapeDtypeStruct((n_streams, CONV_HIST, WA), F32),
        jax.ShapeDtypeStruct((n_streams, POOL_HIST, WC), F32),
        jax.ShapeDtypeStruct((n_streams, vn_rows, WB), F32),
    )
    out_specs = (
        pl.BlockSpec((T, D_MODEL), out_idx),
        pl.BlockSpec((hseg, CONV_HIST, WA), state_idx),
        pl.BlockSpec((hseg, POOL_HIST, WC), state_idx),
        pl.BlockSpec((None, vn_rows, WB) if carry else (hseg, vn_rows, WB), state_idx),
    )
    return pl.pallas_call(
        functools.partial(_mixer_kernel, layer=layer, nseg=nseg, seg_len=seg_len, n_tiles=n_tiles, carry=carry,
                          start=start),
        grid=(WEIGHT_STEPS + total_tiles,),
        in_specs=in_specs,
        out_specs=out_specs,
        out_shape=out_shape,
        scratch_shapes=[pltpu.VMEM(w.shape[1:], BF16) for w in mats] + [
            pltpu.VMEM((hseg, WA // LANES, CONV_PAD + seg_len, LANES), F32),
            pltpu.VMEM((hseg, WC // LANES, POOL_PAD + seg_len, LANES), F32),
        ],
        compiler_params=pltpu.CompilerParams(
            dimension_semantics=("arbitrary",),
            vmem_limit_bytes=V7X_VMEM_LIMIT_BYTES),
        name="mixer_carry" if carry else "mixer_streams",
    )(x, hconv, hpool, *mats, *vecs)


def _ffn_call(tok, n_prompt_rows, pe_inputs, big_w, wts, *, layer, split_out, final):
    split_in = len(tok) == 2
    n_rows = sum(t.shape[0] for t in tok)
    n_p = n_prompt_rows // FFN_TILE
    n_s = (n_rows - n_prompt_rows) // FFN_TILE
    prompt_idx = lambda i: (jnp.clip(i - WEIGHT_STEPS, 0, n_p - 1), 0)
    sample_idx = lambda i: (jnp.clip(i - WEIGHT_STEPS - n_p, 0, n_s - 1), 0)
    all_idx = lambda i: (jnp.clip(i - WEIGHT_STEPS, 0, n_p + n_s - 1), 0)
    pair_specs = [pl.BlockSpec((FFN_TILE, D_MODEL), prompt_idx), pl.BlockSpec((FFN_TILE, D_MODEL), sample_idx)]
    one_spec = [pl.BlockSpec((FFN_TILE, D_MODEL), all_idx)]
    pe_specs = [pl.BlockSpec((None, FFN_TILE, D_PLE), lambda i: (layer,) + prompt_idx(i)),
                pl.BlockSpec((None, FFN_TILE, D_PLE), lambda i: (layer,) + sample_idx(i))]
    with_pe = bool(pe_inputs)
    if split_out:
        out_shape = (jax.ShapeDtypeStruct((n_prompt_rows, D_MODEL), F32),
                     jax.ShapeDtypeStruct((n_rows - n_prompt_rows, D_MODEL), F32))
    else:
        out_shape = (jax.ShapeDtypeStruct((n_rows, D_MODEL), F32),)
    out = pl.pallas_call(
        functools.partial(_ffn_kernel, layer=layer, n_prompt_tiles=n_p, split_in=split_in, split_out=split_out,
                          with_pe=with_pe, final=final),
        grid=(WEIGHT_STEPS + n_p + n_s,),
        in_specs=((pair_specs if split_in else one_spec) + (pe_specs if with_pe else [])
                  + [_chunk_spec(w, layer) for w in big_w] + [_resident(w) for w in wts]),
        out_specs=tuple(pair_specs if split_out else one_spec),
        out_shape=out_shape,
        scratch_shapes=[pltpu.VMEM(w.shape[1:], BF16) for w in big_w],
        compiler_params=pltpu.CompilerParams(
            dimension_semantics=("arbitrary",),
            vmem_limit_bytes=V7X_VMEM_LIMIT_BYTES),
        name="ffn_pe" if with_pe else "ffn",
    )(*tok, *pe_inputs, *big_w, *wts)
    return out if split_out else out[0]


def kernel(x_prompt, x_sample, p_prompt, p_sample, cache_conv, cache_pool, ffn1_norm, ffn1_w_gate, ffn1_w_up, ffn1_w_down, mix_norm, w_in, gate_b, conv_dw_w, conv_dw_b, conv_ln_g, conv_ln_b, conv_w_out, sgu_ln_g, sgu_ln_b, sgu_w_s, sgu_b_s, sgu_w_out, pool_w, pool_scale, pool_w_out, w_out, ffn2_norm, ffn2_w_gate, ffn2_w_up, ffn2_w_down, pe_norm, pe_w_gate, pe_w_proj, final_norm):
    depth = ffn1_norm.shape[0]
    bp, sp, _ = x_prompt.shape
    bs, ss, _ = x_sample.shape
    n_p, n_s = bp * sp, bs * ss
    tok = (x_prompt.reshape(n_p, D_MODEL), x_sample.reshape(n_s, D_MODEL))
    pp_all = p_prompt.reshape(depth, n_p, D_PLE)
    ps_all = p_sample.reshape(depth, n_s, D_PLE)
    conv_zero = jnp.zeros((1, bp, CONV_HIST, WA), F32)
    pool_zero = jnp.zeros((1, bp, POOL_HIST, WC), F32)
    ffn1_mats = (ffn1_w_gate, ffn1_w_up, ffn1_w_down)
    ffn1_vecs = (ffn1_norm,)
    mixer_mats = (w_in, conv_w_out, sgu_w_out, pool_w_out, w_out)
    mixer_vecs = (
        mix_norm, gate_b,
        conv_dw_w, conv_dw_b, conv_ln_g, conv_ln_b,
        sgu_ln_g, sgu_ln_b, sgu_w_s, jnp.swapaxes(sgu_b_s, 1, 2),
        pool_w, pool_scale,
    )
    ffn2_mats = (ffn2_w_gate, ffn2_w_up, ffn2_w_down, pe_w_gate, pe_w_proj)
    ffn2_vecs = (ffn2_norm, pe_norm, final_norm.reshape(1, -1))
    conv_p, conv_s, pool_p, pool_s, v_p, v_s = [], [], [], [], [], []
    for i in range(depth):
        last = i == depth - 1
        h_all = _ffn_call(tok, n_p, (), ffn1_mats, ffn1_vecs, layer=i, split_out=False, final=False)
        hp, cp, pp, vp = _mixer_call(h_all, 0, n_p, sp, conv_zero, pool_zero, mixer_mats, mixer_vecs,
                                     layer=i, hist_layer=0, nseg=1, seg_len=PROMPT_TILE, carry=True, start=0)
        hs, cs, ps, vs = _mixer_call(h_all, n_p, n_s, ss, cache_conv, cache_pool, mixer_mats, mixer_vecs,
                                     layer=i, hist_layer=i, nseg=SAMPLE_STREAMS_PER_TILE, seg_len=ss,
                                     carry=False, start=PAST_LEN)
        out = _ffn_call((hp, hs), n_p, (pp_all, ps_all), ffn2_mats, ffn2_vecs, layer=i, split_out=last,
                        final=last)
        tok = out if last else (out,)

        conv_p.append(cp); conv_s.append(cs); pool_p.append(pp); pool_s.append(ps)
        v_p.append(vp); v_s.append(vs)
    y_prompt, y_sample = tok
    return (y_prompt.reshape(bp, sp, D_MODEL), y_sample.reshape(bs, ss, D_MODEL), jnp.stack(conv_p),
            jnp.stack(conv_s), jnp.stack(pool_p), jnp.stack(pool_s), jnp.stack(v_p), jnp.stack(v_s))
```

```python
import functools

import jax
import jax.numpy as jnp
from jax import lax
from jax.experimental import pallas as pl
from jax.experimental.pallas import tpu as pltpu

D_MODEL = 1024
D_PLE = 256
D_FF = 2816
WA = 512
CONV_W = 31
CONV_HIST = CONV_W - 1
WB = 512
SGU_GROUPS = 4
SGU_CHUNK = 128
SGU_GC = WB // SGU_GROUPS
WC = 512
POOL_WINDOWS = (2, 4, 8, 16)
POOL_GC = WC // len(POOL_WINDOWS)
POOL_HIST = max(POOL_WINDOWS) - 1
D_IN = 2 * WA + 2 * WB + WC + 3 * D_MODEL
PAST_LEN = 2048
EPS = 1e-6

F32 = jnp.float32
BF16 = jnp.bfloat16

V7X_VMEM_LIMIT_BYTES = 60 * 1024 * 1024
LANES = 128
CONV_PAD = 32
POOL_PAD = 16
ROW_BLOCK = 32
GATE_COLS = 512
FF_CHUNKS = ((0, 1024), (1024, 1024), (2048, 768))
WEIGHT_STEPS = 8
N_MIXER_MATS = 5
N_MIXER_VECS = 12

PROMPT_TILE = 256
MIXER_SUB_TILES = 2
SAMPLE_STREAMS_PER_TILE = 4
FFN_TILE = 1024
FFN_PE_TILE = 512
FFN_SUB_TILE = 256


def _rms(x, g):
    return x * lax.rsqrt(jnp.mean(x * x, axis=-1, keepdims=True) + EPS) * g


def _ln(x, g, b):
    mu = jnp.mean(x, axis=-1, keepdims=True)
    xc = x - mu
    return xc * lax.rsqrt(jnp.mean(xc * xc, axis=-1, keepdims=True) + EPS) * g + b


def _dot(a, w):
    return jnp.dot(a, w, preferred_element_type=F32)


def _convert_chunk(step, chunk_ref, dst_ref):
    rows = chunk_ref.shape[0]
    r0 = pl.multiple_of(step * rows, rows)
    dst_ref[pl.ds(r0, rows), :] = chunk_ref[...].astype(BF16)


def _layer_view(ref, layer):
    return ref.at[pl.ds(layer, 1)] if len(ref.shape) == 2 else ref.at[layer]


def _chunk_spec(w, layer):
    rows = w.shape[1] // WEIGHT_STEPS
    return pl.BlockSpec((None, rows, w.shape[2]), lambda i: (layer, jnp.minimum(i, WEIGHT_STEPS - 1), 0))


def _half_swiglu(x, norm_ref, wg_ref, wu_ref, wd_ref):
    n = _rms(x, norm_ref[...]).astype(BF16)
    acc = None
    for c0, cw in FF_CHUNKS:
        g = _dot(n, wg_ref[:, c0:c0 + cw])
        u = _dot(n, wu_ref[:, c0:c0 + cw])
        d = _dot((g * jax.nn.sigmoid(g) * u).astype(BF16), wd_ref[c0:c0 + cw, :])
        acc = d if acc is None else acc + d
    return x + 0.5 * acc


def _mixer_kernel(*refs, layer, n_prompt_tiles, tiles_per_stream, sample_len):
    h_ref, hist_p, hist_s = refs[0], refs[1:3], refs[3:5]
    rest = refs[5:]
    chunks, rest = rest[:N_MIXER_MATS], rest[N_MIXER_MATS:]
    vecs, rest = tuple(_layer_view(v, layer) for v in rest[:N_MIXER_VECS]), rest[N_MIXER_VECS:]
    h_out, state_p, state_s, rest = rest[0], rest[1:4], rest[4:7], rest[7:]
    mats, bufs_p, bufs_s = rest[:N_MIXER_MATS], rest[N_MIXER_MATS:N_MIXER_MATS + 2], rest[N_MIXER_MATS + 2:]
    step = pl.program_id(0)

    def run(tile, hist, state, bufs, *, nseg, seg_len, carry, start):
        T = nseg * seg_len
        conv_out, pool_out, vn_out = state
        for sub in range(MIXER_SUB_TILES):
            rows = pl.ds(sub * T, T)
            segs = pl.ds(0 if carry else sub * nseg, nseg)
            _mixer_tile(tile * MIXER_SUB_TILES + sub, h_ref.at[rows], hist[0].at[segs], hist[1].at[segs],
                        *mats, *vecs, h_out.at[rows], conv_out.at[segs], pool_out.at[segs],
                        vn_out if carry else vn_out.at[segs], bufs[0].at[segs], bufs[1].at[segs],
                        nseg=nseg, seg_len=seg_len, carry=carry, start=start)

    @pl.when(step < WEIGHT_STEPS)
    def _():
        for chunk_ref, dst_ref in zip(chunks, mats):
            _convert_chunk(step, chunk_ref, dst_ref)

    @pl.when(jnp.logical_and(step >= WEIGHT_STEPS, step < WEIGHT_STEPS + n_prompt_tiles))
    def _():
        run(lax.rem(step - WEIGHT_STEPS, tiles_per_stream), hist_p, state_p, bufs_p,
            nseg=1, seg_len=PROMPT_TILE, carry=True, start=0)

    @pl.when(step >= WEIGHT_STEPS + n_prompt_tiles)
    def _():
        run(0, hist_s, state_s, bufs_s,
            nseg=SAMPLE_STREAMS_PER_TILE, seg_len=sample_len, carry=False, start=PAST_LEN)


def _mixer_tile(tile, h_ref, hconv_ref, hpool_ref,
                w_in, conv_wo, sgu_wo, pool_wo, w_out,
                mix_norm, gate_b, dw_w, dw_b, cln_g, cln_b, sln_g, sln_b, sgu_ws, sgu_bst, pool_w, pool_scale,
                h_out, conv_out, pool_out, vn_out, cbuf, pbuf, *, nseg, seg_len, carry, start):
    T = nseg * seg_len
    L = seg_len
    lane_groups = [slice(g * LANES, (g + 1) * LANES) for g in range(WA // LANES)]

    h = h_ref[...]
    n = _rms(h, mix_norm[...]).astype(BF16)

    def zin(c0, cw):
        return _dot(n, w_in[:, c0:c0 + cw])

    def load_history(s):
        for g, lanes in enumerate(lane_groups):
            cbuf[s, g, CONV_PAD - CONV_HIST:CONV_PAD, :] = hconv_ref[s, :, lanes]
            pbuf[s, g, POOL_PAD - POOL_HIST:POOL_PAD, :] = hpool_ref[s, :, lanes]

    if carry:
        pl.when(tile == 0)(functools.partial(load_history, 0))
    else:
        for s in range(nseg):
            load_history(s)

    glu = zin(0, WA) * jax.nn.sigmoid(zin(WA, WA))
    for s in range(nseg):
        for g, lanes in enumerate(lane_groups):
            cbuf[s, g, CONV_PAD:CONV_PAD + L, :] = glu[s * L:(s + 1) * L, lanes]

    def conv_block(s, r0):
        parts = []
        for g, lanes in enumerate(lane_groups):
            acc = None
            for k in range(CONV_W):
                off = CONV_PAD - CONV_HIST + k + r0
                term = dw_w[k:k + 1, lanes] * cbuf[s, g, off:off + ROW_BLOCK, :]
                acc = term if acc is None else acc + term
            parts.append(acc + dw_b[:, lanes])
        return jnp.concatenate(parts, axis=1)

    c_off = 2 * WA + 2 * WB
    g_off = c_off + WC
    gb = gate_b[...]

    def gate_part(j):
        cols = slice(j * GATE_COLS, (j + 1) * GATE_COLS)
        return jax.nn.sigmoid(zin(g_off + j * GATE_COLS, GATE_COLS) + gb[:, cols])

    mxu_tasks = ([functools.partial(zin, c_off, WC), functools.partial(zin, 2 * WA + WB, WB),
                  functools.partial(zin, 2 * WA, WB)]
                 + [functools.partial(gate_part, j) for j in range(3 * D_MODEL // GATE_COLS)])
    blocks = [(s, r0) for s in range(nseg) for r0 in range(0, L, ROW_BLOCK)]
    conv_rows, done = [], []
    for idx, (s, r0) in enumerate(blocks):
        conv_rows.append(conv_block(s, r0))
        while len(done) < (idx + 1) * len(mxu_tasks) // len(blocks):
            done.append(mxu_tasks[len(done)]())
    c, v, u = done[0], done[1], done[2]
    parts_per_gate = D_MODEL // GATE_COLS
    gates = [jnp.concatenate(done[3 + b * parts_per_gate:3 + (b + 1) * parts_per_gate], axis=1)
             for b in range(3)]

    for s in range(nseg):
        for g, lanes in enumerate(lane_groups):
            conv_out[s, :, lanes] = cbuf[s, g, L + CONV_PAD - CONV_HIST:L + CONV_PAD, :]
    if carry:
        for g in range(len(lane_groups)):
            cbuf[0, g, 0:CONV_PAD, :] = cbuf[0, g, L:L + CONV_PAD, :]

    ya = _ln(jnp.concatenate(conv_rows, axis=0), cln_g[...], cln_b[...])
    m = gates[0] * _dot((ya * jax.nn.sigmoid(ya)).astype(BF16), conv_wo[...])

    vn = _ln(v, sln_g[...], sln_b[...])
    if carry:
        vn_out[...] = vn[T - SGU_CHUNK:T]
    else:
        for s in range(nseg):
            vn_out[s] = vn[s * L:(s + 1) * L]
    rows = min(L, SGU_CHUNK)
    ri = lax.broadcasted_iota(jnp.int32, (rows, rows), 0)
    ci = lax.broadcasted_iota(jnp.int32, (rows, rows), 1)
    ws_tril = [jnp.where(ci <= ri, sgu_ws[g, 0:rows, 0:rows], 0.0) for g in range(SGU_GROUPS)]
    bst = sgu_bst[...]
    s_rows = []
    for r0 in range(0, T, rows):
        parts = []
        for g in range(SGU_GROUPS):
            sg = _dot(ws_tril[g], vn[r0:r0 + rows, g * SGU_GC:(g + 1) * SGU_GC])
            parts.append(sg + bst[0:rows, g:g + 1])
        s_rows.append(jnp.concatenate(parts, axis=1))
    m = m + gates[1] * _dot((u * jnp.concatenate(s_rows, axis=0)).astype(BF16), sgu_wo[...])

    for s in range(nseg):
        for g, lanes in enumerate(lane_groups):
            pbuf[s, g, POOL_PAD:POOL_PAD + L, :] = c[s * L:(s + 1) * L, lanes]
    pos0 = start + (tile * L if carry else 0)
    pos = pos0 + lax.broadcasted_iota(jnp.int32, (L, 1), 0)
    inv_cnt = [1.0 / jnp.minimum(pos + 1, w).astype(F32) for w in POOL_WINDOWS]
    d_rows = []
    for s in range(nseg):
        for r0 in range(0, L, ROW_BLOCK):
            parts = []
            for g, w in enumerate(POOL_WINDOWS):
                tot = None
                for j in range(w):
                    lo = POOL_PAD - j + r0
                    term = pbuf[s, g, lo:lo + ROW_BLOCK, :]
                    tot = term if tot is None else tot + term
                cur = pbuf[s, g, POOL_PAD + r0:POOL_PAD + r0 + ROW_BLOCK, :]
                parts.append(tot * inv_cnt[g][r0:r0 + ROW_BLOCK] - cur)
            d_rows.append(jnp.concatenate(parts, axis=1))
        for g, lanes in enumerate(lane_groups):
            pool_out[s, :, lanes] = pbuf[s, g, L + POOL_PAD - POOL_HIST:L + POOL_PAD, :]
    if carry:
        for g in range(len(lane_groups)):
            pbuf[0, g, 0:POOL_PAD, :] = pbuf[0, g, L:L + POOL_PAD, :]
    dd = jnp.concatenate(d_rows, axis=0)
    yc = jnp.concatenate(
        [_dot(dd[:, g * POOL_GC:(g + 1) * POOL_GC], pool_w[g]) for g in range(len(POOL_WINDOWS))], axis=1)
    m = m + gates[2] * _dot((yc * pool_scale[...]).astype(BF16), pool_wo[...])

    h_out[...] = h + _dot(m.astype(BF16), w_out[...])


def _ffn_kernel(*refs, layer, n_prompt_tiles, split_in, split_out, with_pe, final):
    n_mats = 5 if with_pe else 3
    n_in = 2 if split_in else 1
    n_out = 2 if split_out else 1
    tok_in, rest = refs[:n_in], refs[n_in:]
    if with_pe:
        (pp_ref, ps_ref), rest = rest[:2], rest[2:]
    chunks, rest = rest[:n_mats], rest[n_mats:]
    mats = rest[-n_mats:]
    vecs, tok_out = rest[:-n_mats - n_out], rest[-n_mats - n_out:-n_mats]
    if with_pe:
        f_norm, pe_norm, fin_norm = vecs
        f_wg, f_wu, f_wd, pe_wg, pe_wp = mats
        pe_norm = _layer_view(pe_norm, layer)
    else:
        f_norm, = vecs
        f_wg, f_wu, f_wd = mats
    f_norm = _layer_view(f_norm, layer)
    step = pl.program_id(0)

    @pl.when(step < WEIGHT_STEPS)
    def _():
        for chunk_ref, dst_ref in zip(chunks, mats):
            _convert_chunk(step, chunk_ref, dst_ref)

    @pl.when(step >= WEIGHT_STEPS)
    def _():
        is_prompt = step < WEIGHT_STEPS + n_prompt_tiles

        def pick(prompt_ref, sample_ref, rs):
            return jnp.where(is_prompt, prompt_ref[rs, :], sample_ref[rs, :])

        sub_tiles = [slice(r0, r0 + FFN_SUB_TILE) for r0 in range(0, tok_out[0].shape[0], FFN_SUB_TILE)]
        results = [_half_swiglu(pick(*tok_in, rs) if split_in else tok_in[0][rs, :], f_norm, f_wg, f_wu, f_wd)
                   for rs in sub_tiles]
        if with_pe:
            gated = []
            for rs, h in zip(sub_tiles, results):
                gate = jax.nn.sigmoid(_dot(_rms(h, pe_norm[...]).astype(BF16), pe_wg[...]))
                gated.append(h + gate * _dot(pick(pp_ref, ps_ref, rs).astype(BF16), pe_wp[...]))
            results = gated
        if final:
            results = [_rms(h, fin_norm[...]) for h in results]

        def store(o_ref):
            for rs, h in zip(sub_tiles, results):
                o_ref[rs, :] = h

        if split_out:
            pl.when(is_prompt)(functools.partial(store, tok_out[0]))
            pl.when(jnp.logical_not(is_prompt))(functools.partial(store, tok_out[1]))
        else:
            store(tok_out[0])


def _resident(arr):
    del arr
    return pl.BlockSpec(memory_space=pltpu.VMEM)


def _mixer_call(x, n_prompt_rows, prompt_len, sample_len, hist_p, hist_s, mats, vecs, *, layer):
    assert len(mats) == N_MIXER_MATS and len(vecs) == N_MIXER_VECS
    n_rows = x.shape[0]
    T = MIXER_SUB_TILES * PROMPT_TILE
    sseg = MIXER_SUB_TILES * SAMPLE_STREAMS_PER_TILE
    assert sseg * sample_len == T
    tiles_per_stream = prompt_len // T
    n_p_tiles = n_prompt_rows // T
    n_s_tiles = (n_rows - n_prompt_rows) // T
    n_p_streams, n_s_streams = hist_p[0].shape[1], hist_s[0].shape[1]

    def tile_idx(i):
        return jnp.clip(i - WEIGHT_STEPS, 0, n_p_tiles + n_s_tiles - 1)

    def p_idx(i):
        return jnp.minimum(tile_idx(i) // tiles_per_stream, n_p_streams - 1), 0, 0

    def s_idx(i):
        return jnp.maximum(tile_idx(i) - n_p_tiles, 0), 0, 0

    tok_spec = pl.BlockSpec((T, D_MODEL), lambda i: (tile_idx(i), 0))
    in_specs = [
        tok_spec,
        pl.BlockSpec((None, 1, CONV_HIST, WA), lambda i: (0,) + p_idx(i)),
        pl.BlockSpec((None, 1, POOL_HIST, WC), lambda i: (0,) + p_idx(i)),
        pl.BlockSpec((None, sseg, CONV_HIST, WA), lambda i: (layer,) + s_idx(i)),
        pl.BlockSpec((None, sseg, POOL_HIST, WC), lambda i: (layer,) + s_idx(i)),
    ] + [_chunk_spec(w, layer) for w in mats] + [_resident(w) for w in vecs]
    out_shape = (
        jax.ShapeDtypeStruct((n_rows, D_MODEL), F32),
        jax.ShapeDtypeStruct((n_p_streams, CONV_HIST, WA), F32),
        jax.ShapeDtypeStruct((n_p_streams, POOL_HIST, WC), F32),
        jax.ShapeDtypeStruct((n_p_streams, SGU_CHUNK, WB), F32),
        jax.ShapeDtypeStruct((n_s_streams, CONV_HIST, WA), F32),
        jax.ShapeDtypeStruct((n_s_streams, POOL_HIST, WC), F32),
        jax.ShapeDtypeStruct((n_s_streams, sample_len, WB), F32),
    )
    out_specs = (
        tok_spec,
        pl.BlockSpec((1, CONV_HIST, WA), p_idx),
        pl.BlockSpec((1, POOL_HIST, WC), p_idx),
        pl.BlockSpec((None, SGU_CHUNK, WB), p_idx),
        pl.BlockSpec((sseg, CONV_HIST, WA), s_idx),
        pl.BlockSpec((sseg, POOL_HIST, WC), s_idx),
        pl.BlockSpec((sseg, sample_len, WB), s_idx),
    )
    out = pl.pallas_call(
        functools.partial(_mixer_kernel, layer=layer, n_prompt_tiles=n_p_tiles, tiles_per_stream=tiles_per_stream,
                          sample_len=sample_len),
        grid=(WEIGHT_STEPS + n_p_tiles + n_s_tiles,),
        in_specs=in_specs,
        out_specs=out_specs,
        out_shape=out_shape,
        scratch_shapes=[pltpu.VMEM(w.shape[1:], BF16) for w in mats] + [
            pltpu.VMEM((1, WA // LANES, CONV_PAD + PROMPT_TILE, LANES), F32),
            pltpu.VMEM((1, WC // LANES, POOL_PAD + PROMPT_TILE, LANES), F32),
            pltpu.VMEM((sseg, WA // LANES, CONV_PAD + sample_len, LANES), F32),
            pltpu.VMEM((sseg, WC // LANES, POOL_PAD + sample_len, LANES), F32),
        ],
        compiler_params=pltpu.CompilerParams(
            dimension_semantics=("arbitrary",),
            vmem_limit_bytes=V7X_VMEM_LIMIT_BYTES),
        name="mixer",
    )(x, *hist_p, *hist_s, *mats, *vecs)
    return out[0], out[1:4], out[4:7]


def _ffn_call(tok, n_prompt_rows, pe_inputs, big_w, wts, *, layer, split_out, final):
    split_in = len(tok) == 2
    with_pe = bool(pe_inputs)
    tile = FFN_PE_TILE if with_pe else FFN_TILE
    n_rows = sum(t.shape[0] for t in tok)
    n_p = n_prompt_rows // tile
    n_s = (n_rows - n_prompt_rows) // tile
    prompt_idx = lambda i: (jnp.clip(i - WEIGHT_STEPS, 0, n_p - 1), 0)
    sample_idx = lambda i: (jnp.clip(i - WEIGHT_STEPS - n_p, 0, n_s - 1), 0)
    all_idx = lambda i: (jnp.clip(i - WEIGHT_STEPS, 0, n_p + n_s - 1), 0)
    pair_specs = [pl.BlockSpec((tile, D_MODEL), prompt_idx), pl.BlockSpec((tile, D_MODEL), sample_idx)]
    one_spec = [pl.BlockSpec((tile, D_MODEL), all_idx)]
    pe_specs = [pl.BlockSpec((None, tile, D_PLE), lambda i: (layer,) + prompt_idx(i)),
                pl.BlockSpec((None, tile, D_PLE), lambda i: (layer,) + sample_idx(i))]
    if split_out:
        out_shape = (jax.ShapeDtypeStruct((n_prompt_rows, D_MODEL), F32),
                     jax.ShapeDtypeStruct((n_rows - n_prompt_rows, D_MODEL), F32))
    else:
        out_shape = (jax.ShapeDtypeStruct((n_rows, D_MODEL), F32),)
    out = pl.pallas_call(
        functools.partial(_ffn_kernel, layer=layer, n_prompt_tiles=n_p, split_in=split_in, split_out=split_out,
                          with_pe=with_pe, final=final),
        grid=(WEIGHT_STEPS + n_p + n_s,),
        in_specs=((pair_specs if split_in else one_spec) + (pe_specs if with_pe else [])
                  + [_chunk_spec(w, layer) for w in big_w] + [_resident(w) for w in wts]),
        out_specs=tuple(pair_specs if split_out else one_spec),
        out_shape=out_shape,
        scratch_shapes=[pltpu.VMEM(w.shape[1:], BF16) for w in big_w],
        compiler_params=pltpu.CompilerParams(
            dimension_semantics=("arbitrary",),
            vmem_limit_bytes=V7X_VMEM_LIMIT_BYTES),
        name="ffn_pe" if with_pe else "ffn",
    )(*tok, *pe_inputs, *big_w, *wts)
    return out if split_out else out[0]


def kernel(x_prompt, x_sample, p_prompt, p_sample, cache_conv, cache_pool, ffn1_norm, ffn1_w_gate, ffn1_w_up, ffn1_w_down, mix_norm, w_in, gate_b, conv_dw_w, conv_dw_b, conv_ln_g, conv_ln_b, conv_w_out, sgu_ln_g, sgu_ln_b, sgu_w_s, sgu_b_s, sgu_w_out, pool_w, pool_scale, pool_w_out, w_out, ffn2_norm, ffn2_w_gate, ffn2_w_up, ffn2_w_down, pe_norm, pe_w_gate, pe_w_proj, final_norm):
    depth = ffn1_norm.shape[0]
    bp, sp, _ = x_prompt.shape
    bs, ss, _ = x_sample.shape
    n_p, n_s = bp * sp, bs * ss
    tok = (x_prompt.reshape(n_p, D_MODEL), x_sample.reshape(n_s, D_MODEL))
    pp_all = p_prompt.reshape(depth, n_p, D_PLE)
    ps_all = p_sample.reshape(depth, n_s, D_PLE)
    conv_zero = jnp.zeros((1, bp, CONV_HIST, WA), F32)
    pool_zero = jnp.zeros((1, bp, POOL_HIST, WC), F32)
    ffn1_mats = (ffn1_w_gate, ffn1_w_up, ffn1_w_down)
    ffn1_vecs = (ffn1_norm,)
    mixer_mats = (w_in, conv_w_out, sgu_w_out, pool_w_out, w_out)
    mixer_vecs = (
        mix_norm, gate_b,
        conv_dw_w, conv_dw_b, conv_ln_g, conv_ln_b,
        sgu_ln_g, sgu_ln_b, sgu_w_s, jnp.swapaxes(sgu_b_s, 1, 2),
        pool_w, pool_scale,
    )
    ffn2_mats = (ffn2_w_gate, ffn2_w_up, ffn2_w_down, pe_w_gate, pe_w_proj)
    ffn2_vecs = (ffn2_norm, pe_norm, final_norm.reshape(1, -1))
    conv_p, conv_s, pool_p, pool_s, v_p, v_s = [], [], [], [], [], []
    for i in range(depth):
        last = i == depth - 1
        h_all = _ffn_call(tok, n_p, (), ffn1_mats, ffn1_vecs, layer=i, split_out=False, final=False)
        h_all, (cp, pp, vp), (cs, ps, vs) = _mixer_call(h_all, n_p, sp, ss, (conv_zero, pool_zero),
                                                        (cache_conv, cache_pool), mixer_mats, mixer_vecs, layer=i)
        out = _ffn_call((h_all,), n_p, (pp_all, ps_all), ffn2_mats, ffn2_vecs, layer=i, split_out=last,
                        final=last)
        tok = out if last else (out,)

        conv_p.append(cp); conv_s.append(cs); pool_p.append(pp); pool_s.append(ps)
        v_p.append(vp); v_s.append(vs)
    y_prompt, y_sample = tok
    return (y_prompt.reshape(bp, sp, D_MODEL), y_sample.reshape(bs, ss, D_MODEL), jnp.stack(conv_p),
            jnp.stack(conv_s), jnp.stack(pool_p), jnp.stack(pool_s), jnp.stack(v_p), jnp.stack(v_s))
```

```python
import functools

import jax
import jax.numpy as jnp
from jax import lax
from jax.experimental import pallas as pl
from jax.experimental.pallas import tpu as pltpu

D_MODEL = 1024
D_PLE = 256
D_FF = 2816
WA = 512
CONV_W = 31
CONV_HIST = CONV_W - 1
WB = 512
SGU_GROUPS = 4
SGU_CHUNK = 128
SGU_GC = WB // SGU_GROUPS
WC = 512
POOL_WINDOWS = (2, 4, 8, 16)
POOL_GC = WC // len(POOL_WINDOWS)
POOL_HIST = max(POOL_WINDOWS) - 1
D_IN = 2 * WA + 2 * WB + WC + 3 * D_MODEL
PAST_LEN = 2048
EPS = 1e-6

F32 = jnp.float32
BF16 = jnp.bfloat16

V7X_VMEM_LIMIT_BYTES = 60 * 1024 * 1024
LANES = 128
CONV_PAD = 32
POOL_PAD = 16
ROW_BLOCK = 32
GATE_COLS = 512
FF_CHUNKS = ((0, 1024), (1024, 1024), (2048, 768))
WEIGHT_STEPS = 8
N_MIXER_MATS = 5
N_MIXER_VECS = 12

PROMPT_TILE = 256
MIXER_SUB_TILES = 2
SAMPLE_STREAMS_PER_TILE = 4
FFN_TILE = 1024
FFN_PE_TILE = 512
FFN_SUB_TILE = 256


def _rms(x, g):
    return x * lax.rsqrt(jnp.mean(x * x, axis=-1, keepdims=True) + EPS) * g


def _ln(x, g, b):
    mu = jnp.mean(x, axis=-1, keepdims=True)
    xc = x - mu
    return xc * lax.rsqrt(jnp.mean(xc * xc, axis=-1, keepdims=True) + EPS) * g + b


def _dot(a, w):
    return jnp.dot(a, w, preferred_element_type=F32)


def _convert_chunk(step, chunk_ref, dst_ref):
    rows = chunk_ref.shape[0]
    r0 = pl.multiple_of(step * rows, rows)
    dst_ref[pl.ds(r0, rows), :] = chunk_ref[...].astype(BF16)


def _layer_view(ref, layer):
    return ref.at[pl.ds(layer, 1)] if len(ref.shape) == 2 else ref.at[layer]


def _chunk_spec(w, layer):
    rows = w.shape[1] // WEIGHT_STEPS
    return pl.BlockSpec((None, rows, w.shape[2]), lambda i: (layer, jnp.minimum(i, WEIGHT_STEPS - 1), 0))


def _half_swiglu(x, norm_ref, wg_ref, wu_ref, wd_ref):
    n = _rms(x, norm_ref[...]).astype(BF16)
    acc = None
    for c0, cw in FF_CHUNKS:
        g = _dot(n, wg_ref[:, c0:c0 + cw])
        u = _dot(n, wu_ref[:, c0:c0 + cw])
        d = _dot((g * jax.nn.sigmoid(g) * u).astype(BF16), wd_ref[c0:c0 + cw, :])
        acc = d if acc is None else acc + d
    return x + 0.5 * acc


def _mixer_kernel(*refs, layer, n_prompt_tiles, tiles_per_stream, sample_len):
    h_ref, hist_p, hist_s = refs[0], refs[1:3], refs[3:5]
    rest = refs[5:]
    chunks, rest = rest[:N_MIXER_MATS], rest[N_MIXER_MATS:]
    vecs, rest = tuple(_layer_view(v, layer) for v in rest[:N_MIXER_VECS]), rest[N_MIXER_VECS:]
    h_out, state_p, state_s, rest = rest[0], rest[1:4], rest[4:7], rest[7:]
    mats, bufs_p, bufs_s = rest[:N_MIXER_MATS], rest[N_MIXER_MATS:N_MIXER_MATS + 2], rest[N_MIXER_MATS + 2:]
    step = pl.program_id(0)

    def run(tile, hist, state, bufs, *, nseg, seg_len, carry, start):
        T = nseg * seg_len
        conv_out, pool_out, vn_out = state
        for sub in range(MIXER_SUB_TILES):
            rows = pl.ds(sub * T, T)
            segs = pl.ds(0 if carry else sub * nseg, nseg)
            _mixer_tile(tile * MIXER_SUB_TILES + sub, h_ref.at[rows], hist[0].at[segs], hist[1].at[segs],
                        *mats, *vecs, h_out.at[rows], conv_out.at[segs], pool_out.at[segs],
                        vn_out if carry else vn_out.at[segs], bufs[0].at[segs], bufs[1].at[segs],
                        nseg=nseg, seg_len=seg_len, carry=carry, start=start)

    @pl.when(step < WEIGHT_STEPS)
    def _():
        for chunk_ref, dst_ref in zip(chunks, mats):
            _convert_chunk(step, chunk_ref, dst_ref)

    @pl.when(jnp.logical_and(step >= WEIGHT_STEPS, step < WEIGHT_STEPS + n_prompt_tiles))
    def _():
        run(lax.rem(step - WEIGHT_STEPS, tiles_per_stream), hist_p, state_p, bufs_p,
            nseg=1, seg_len=PROMPT_TILE, carry=True, start=0)

    @pl.when(step >= WEIGHT_STEPS + n_prompt_tiles)
    def _():
        run(0, hist_s, state_s, bufs_s,
            nseg=SAMPLE_STREAMS_PER_TILE, seg_len=sample_len, carry=False, start=PAST_LEN)


def _mixer_tile(tile, h_ref, hconv_ref, hpool_ref,
                w_in, conv_wo, sgu_wo, pool_wo, w_out,
                mix_norm, gate_b, dw_w, dw_b, cln_g, cln_b, sln_g, sln_b, sgu_ws, sgu_bst, pool_w, pool_scale,
                h_out, conv_out, pool_out, vn_out, cbuf, pbuf, *, nseg, seg_len, carry, start):
    T = nseg * seg_len
    L = seg_len
    lane_groups = [slice(g * LANES, (g + 1) * LANES) for g in range(WA // LANES)]

    h = h_ref[...]
    n = _rms(h, mix_norm[...]).astype(BF16)

    def zin(c0, cw):
        return _dot(n, w_in[:, c0:c0 + cw])

    def load_history(s):
        for g, lanes in enumerate(lane_groups):
            cbuf[s, g, CONV_PAD - CONV_HIST:CONV_PAD, :] = hconv_ref[s, :, lanes]
            pbuf[s, g, POOL_PAD - POOL_HIST:POOL_PAD, :] = hpool_ref[s, :, lanes]

    if carry:
        pl.when(tile == 0)(functools.partial(load_history, 0))
    else:
        for s in range(nseg):
            load_history(s)

    glu = zin(0, WA) * jax.nn.sigmoid(zin(WA, WA))
    for s in range(nseg):
        for g, lanes in enumerate(lane_groups):
            cbuf[s, g, CONV_PAD:CONV_PAD + L, :] = glu[s * L:(s + 1) * L, lanes]

    def conv_block(s, r0):
        parts = []
        for g, lanes in enumerate(lane_groups):
            acc = None
            for k in range(CONV_W):
                off = CONV_PAD - CONV_HIST + k + r0
                term = dw_w[k:k + 1, lanes] * cbuf[s, g, off:off + ROW_BLOCK, :]
                acc = term if acc is None else acc + term
            parts.append(acc + dw_b[:, lanes])
        return jnp.concatenate(parts, axis=1)

    c_off = 2 * WA + 2 * WB
    g_off = c_off + WC
    gb = gate_b[...]

    def gate_part(j):
        cols = slice(j * GATE_COLS, (j + 1) * GATE_COLS)
        return jax.nn.sigmoid(zin(g_off + j * GATE_COLS, GATE_COLS) + gb[:, cols])

    mxu_tasks = ([functools.partial(zin, c_off, WC), functools.partial(zin, 2 * WA + WB, WB),
                  functools.partial(zin, 2 * WA, WB)]
                 + [functools.partial(gate_part, j) for j in range(3 * D_MODEL // GATE_COLS)])
    blocks = [(s, r0) for s in range(nseg) for r0 in range(0, L, ROW_BLOCK)]
    conv_rows, done = [], []
    for idx, (s, r0) in enumerate(blocks):
        conv_rows.append(conv_block(s, r0))
        while len(done) < (idx + 1) * len(mxu_tasks) // len(blocks):
            done.append(mxu_tasks[len(done)]())
    c, v, u = done[0], done[1], done[2]
    parts_per_gate = D_MODEL // GATE_COLS
    gates = [jnp.concatenate(done[3 + b * parts_per_gate:3 + (b + 1) * parts_per_gate], axis=1)
             for b in range(3)]

    for s in range(nseg):
        for g, lanes in enumerate(lane_groups):
            conv_out[s, :, lanes] = cbuf[s, g, L + CONV_PAD - CONV_HIST:L + CONV_PAD, :]
    if carry:
        for g in range(len(lane_groups)):
            cbuf[0, g, 0:CONV_PAD, :] = cbuf[0, g, L:L + CONV_PAD, :]

    ya = _ln(jnp.concatenate(conv_rows, axis=0), cln_g[...], cln_b[...])
    m = gates[0] * _dot((ya * jax.nn.sigmoid(ya)).astype(BF16), conv_wo[...])

    vn = _ln(v, sln_g[...], sln_b[...])
    if carry:
        vn_out[...] = vn[T - SGU_CHUNK:T]
    else:
        for s in range(nseg):
            vn_out[s] = vn[s * L:(s + 1) * L]
    rows = min(L, SGU_CHUNK)
    ri = lax.broadcasted_iota(jnp.int32, (rows, rows), 0)
    ci = lax.broadcasted_iota(jnp.int32, (rows, rows), 1)
    ws_tril = [jnp.where(ci <= ri, sgu_ws[g, 0:rows, 0:rows], 0.0) for g in range(SGU_GROUPS)]
    bst = sgu_bst[...]
    s_rows = []
    for r0 in range(0, T, rows):
        parts = []
        for g in range(SGU_GROUPS):
            sg = _dot(ws_tril[g], vn[r0:r0 + rows, g * SGU_GC:(g + 1) * SGU_GC])
            parts.append(sg + bst[0:rows, g:g + 1])
        s_rows.append(jnp.concatenate(parts, axis=1))
    m = m + gates[1] * _dot((u * jnp.concatenate(s_rows, axis=0)).astype(BF16), sgu_wo[...])

    for s in range(nseg):
        for g, lanes in enumerate(lane_groups):
            pbuf[s, g, POOL_PAD:POOL_PAD + L, :] = c[s * L:(s + 1) * L, lanes]
    pos0 = start + (tile * L if carry else 0)
    pos = pos0 + lax.broadcasted_iota(jnp.int32, (L, 1), 0)
    inv_cnt = [1.0 / jnp.minimum(pos + 1, w).astype(F32) for w in POOL_WINDOWS]
    d_rows = []
    for s in range(nseg):
        for r0 in range(0, L, ROW_BLOCK):
            parts = []
            for g, w in enumerate(POOL_WINDOWS):
                tot = None
                for j in range(w):
                    lo = POOL_PAD - j + r0
                    term = pbuf[s, g, lo:lo + ROW_BLOCK, :]
                    tot = term if tot is None else tot + term
                cur = pbuf[s, g, POOL_PAD + r0:POOL_PAD + r0 + ROW_BLOCK, :]
                parts.append(tot * inv_cnt[g][r0:r0 + ROW_BLOCK] - cur)
            d_rows.append(jnp.concatenate(parts, axis=1))
        for g, lanes in enumerate(lane_groups):
            pool_out[s, :, lanes] = pbuf[s, g, L + POOL_PAD - POOL_HIST:L + POOL_PAD, :]
    if carry:
        for g in range(len(lane_groups)):
            pbuf[0, g, 0:POOL_PAD, :] = pbuf[0, g, L:L + POOL_PAD, :]
    dd = jnp.concatenate(d_rows, axis=0)
    yc = jnp.concatenate(
        [_dot(dd[:, g * POOL_GC:(g + 1) * POOL_GC], pool_w[g]) for g in range(len(POOL_WINDOWS))], axis=1)
    m = m + gates[2] * _dot((yc * pool_scale[...]).astype(BF16), pool_wo[...])

    h_out[...] = h + _dot(m.astype(BF16), w_out[...])


def _ffn_kernel(*refs, layer, n_prompt_tiles, split_in, split_out, with_pe, final):
    n_mats = 5 if with_pe else 3
    n_in = 2 if split_in else 1
    n_out = 2 if split_out else 1
    tok_in, rest = refs[:n_in], refs[n_in:]
    if with_pe:
        (pp_ref, ps_ref), rest = rest[:2], rest[2:]
    chunks, rest = rest[:n_mats], rest[n_mats:]
    mats = rest[-n_mats:]
    vecs, tok_out = rest[:-n_mats - n_out], rest[-n_mats - n_out:-n_mats]
    if with_pe:
        f_norm, pe_norm, fin_norm = vecs
        f_wg, f_wu, f_wd, pe_wg, pe_wp = mats
        pe_norm = _layer_view(pe_norm, layer)
    else:
        f_norm, = vecs
        f_wg, f_wu, f_wd = mats
    f_norm = _layer_view(f_norm, layer)
    step = pl.program_id(0)

    @pl.when(step < WEIGHT_STEPS)
    def _():
        for chunk_ref, dst_ref in zip(chunks, mats):
            _convert_chunk(step, chunk_ref, dst_ref)

    @pl.when(step >= WEIGHT_STEPS)
    def _():
        is_prompt = step < WEIGHT_STEPS + n_prompt_tiles

        def pick(prompt_ref, sample_ref, rs):
            return jnp.where(is_prompt, prompt_ref[rs, :], sample_ref[rs, :])

        sub_tiles = [slice(r0, r0 + FFN_SUB_TILE) for r0 in range(0, tok_out[0].shape[0], FFN_SUB_TILE)]
        results = [_half_swiglu(pick(*tok_in, rs) if split_in else tok_in[0][rs, :], f_norm, f_wg, f_wu, f_wd)
                   for rs in sub_tiles]
        if with_pe:
            gated = []
            for rs, h in zip(sub_tiles, results):
                gate = jax.nn.sigmoid(_dot(_rms(h, pe_norm[...]).astype(BF16), pe_wg[...]))
                gated.append(h + gate * _dot(pick(pp_ref, ps_ref, rs).astype(BF16), pe_wp[...]))
            results = gated
        if final:
            results = [_rms(h, fin_norm[...]) for h in results]

        def store(o_ref):
            for rs, h in zip(sub_tiles, results):
                o_ref[rs, :] = h

        if split_out:
            pl.when(is_prompt)(functools.partial(store, tok_out[0]))
            pl.when(jnp.logical_not(is_prompt))(functools.partial(store, tok_out[1]))
        else:
            store(tok_out[0])


def _resident(arr):
    del arr
    return pl.BlockSpec(memory_space=pltpu.VMEM)


def _mixer_call(x, n_prompt_rows, prompt_len, sample_len, hist_p, hist_s, mats, vecs, *, layer):
    assert len(mats) == N_MIXER_MATS and len(vecs) == N_MIXER_VECS
    n_rows = x.shape[0]
    T = MIXER_SUB_TILES * PROMPT_TILE
    sseg = MIXER_SUB_TILES * SAMPLE_STREAMS_PER_TILE
    assert sseg * sample_len == T
    tiles_per_stream = prompt_len // T
    n_p_tiles = n_prompt_rows // T
    n_s_tiles = (n_rows - n_prompt_rows) // T
    n_p_streams, n_s_streams = hist_p[0].shape[1], hist_s[0].shape[1]

    def tile_idx(i):
        return jnp.clip(i - WEIGHT_STEPS, 0, n_p_tiles + n_s_tiles - 1)

    def p_idx(i):
        return jnp.minimum(tile_idx(i) // tiles_per_stream, n_p_streams - 1), 0, 0

    def s_idx(i):
        return jnp.maximum(tile_idx(i) - n_p_tiles, 0), 0, 0

    tok_spec = pl.BlockSpec((T, D_MODEL), lambda i: (tile_idx(i), 0))
    in_specs = [
        tok_spec,
        pl.BlockSpec((None, 1, CONV_HIST, WA), lambda i: (0,) + p_idx(i)),
        pl.BlockSpec((None, 1, POOL_HIST, WC), lambda i: (0,) + p_idx(i)),
        pl.BlockSpec((None, sseg, CONV_HIST, WA), lambda i: (layer,) + s_idx(i)),
        pl.BlockSpec((None, sseg, POOL_HIST, WC), lambda i: (layer,) + s_idx(i)),
    ] + [_chunk_spec(w, layer) for w in mats] + [_resident(w) for w in vecs]
    out_shape = (
        jax.ShapeDtypeStruct((n_rows, D_MODEL), F32),
        jax.ShapeDtypeStruct((n_p_streams, CONV_HIST, WA), F32),
        jax.ShapeDtypeStruct((n_p_streams, POOL_HIST, WC), F32),
        jax.ShapeDtypeStruct((n_p_streams, SGU_CHUNK, WB), F32),
        jax.ShapeDtypeStruct((n_s_streams, CONV_HIST, WA), F32),
        jax.ShapeDtypeStruct((n_s_streams, POOL_HIST, WC), F32),
        jax.ShapeDtypeStruct((n_s_streams, sample_len, WB), F32),
    )
    out_specs = (
        tok_spec,
        pl.BlockSpec((1, CONV_HIST, WA), p_idx),
        pl.BlockSpec((1, POOL_HIST, WC), p_idx),
        pl.BlockSpec((None, SGU_CHUNK, WB), p_idx),
        pl.BlockSpec((sseg, CONV_HIST, WA), s_idx),
        pl.BlockSpec((sseg, POOL_HIST, WC), s_idx),
        pl.BlockSpec((sseg, sample_len, WB), s_idx),
    )
    out = pl.pallas_call(
        functools.partial(_mixer_kernel, layer=layer, n_prompt_tiles=n_p_tiles, tiles_per_stream=tiles_per_stream,
                          sample_len=sample_len),
        grid=(WEIGHT_STEPS + n_p_tiles + n_s_tiles,),
        in_specs=in_specs,
        out_specs=out_specs,
        out_shape=out_shape,
        scratch_shapes=[pltpu.VMEM(w.shape[1:], BF16) for w in mats] + [
            pltpu.VMEM((1, WA // LANES, CONV_PAD + PROMPT_TILE, LANES), F32),
            pltpu.VMEM((1, WC // LANES, POOL_PAD + PROMPT_TILE, LANES), F32),
            pltpu.VMEM((sseg, WA // LANES, CONV_PAD + sample_len, LANES), F32),
            pltpu.VMEM((sseg, WC // LANES, POOL_PAD + sample_len, LANES), F32),
        ],
        compiler_params=pltpu.CompilerParams(
            dimension_semantics=("arbitrary",),
            vmem_limit_bytes=V7X_VMEM_LIMIT_BYTES),
        name="mixer",
    )(x, *hist_p, *hist_s, *mats, *vecs)
    return out[0], out[1:4], out[4:7]


def _ffn_call(tok, n_prompt_rows, pe_inputs, big_w, wts, *, layer, split_out, final):
    split_in = len(tok) == 2
    with_pe = bool(pe_inputs)
    tile = FFN_PE_TILE if (with_pe and split_out) else FFN_TILE
    n_rows = sum(t.shape[0] for t in tok)
    n_p = n_prompt_rows // tile
    n_s = (n_rows - n_prompt_rows) // tile
    prompt_idx = lambda i: (jnp.clip(i - WEIGHT_STEPS, 0, n_p - 1), 0)
    sample_idx = lambda i: (jnp.clip(i - WEIGHT_STEPS - n_p, 0, n_s - 1), 0)
    all_idx = lambda i: (jnp.clip(i - WEIGHT_STEPS, 0, n_p + n_s - 1), 0)
    pair_specs = [pl.BlockSpec((tile, D_MODEL), prompt_idx), pl.BlockSpec((tile, D_MODEL), sample_idx)]
    one_spec = [pl.BlockSpec((tile, D_MODEL), all_idx)]
    pe_specs = [pl.BlockSpec((None, tile, D_PLE), lambda i: (layer,) + prompt_idx(i)),
                pl.BlockSpec((None, tile, D_PLE), lambda i: (layer,) + sample_idx(i))]
    if split_out:
        out_shape = (jax.ShapeDtypeStruct((n_prompt_rows, D_MODEL), F32),
                     jax.ShapeDtypeStruct((n_rows - n_prompt_rows, D_MODEL), F32))
    else:
        out_shape = (jax.ShapeDtypeStruct((n_rows, D_MODEL), F32),)
    out = pl.pallas_call(
        functools.partial(_ffn_kernel, layer=layer, n_prompt_tiles=n_p, split_in=split_in, split_out=split_out,
                          with_pe=with_pe, final=final),
        grid=(WEIGHT_STEPS + n_p + n_s,),
        in_specs=((pair_specs if split_in else one_spec) + (pe_specs if with_pe else [])
                  + [_chunk_spec(w, layer) for w in big_w] + [_resident(w) for w in wts]),
        out_specs=tuple(pair_specs if split_out else one_spec),
        out_shape=out_shape,
        scratch_shapes=[pltpu.VMEM(w.shape[1:], BF16) for w in big_w],
        compiler_params=pltpu.CompilerParams(
            dimension_semantics=("arbitrary",),
            vmem_limit_bytes=V7X_VMEM_LIMIT_BYTES),
        name="ffn_pe" if with_pe else "ffn",
    )(*tok, *pe_inputs, *big_w, *wts)
    return out if split_out else out[0]


def kernel(x_prompt, x_sample, p_prompt, p_sample, cache_conv, cache_pool, ffn1_norm, ffn1_w_gate, ffn1_w_up, ffn1_w_down, mix_norm, w_in, gate_b, conv_dw_w, conv_dw_b, conv_ln_g, conv_ln_b, conv_w_out, sgu_ln_g, sgu_ln_b, sgu_w_s, sgu_b_s, sgu_w_out, pool_w, pool_scale, pool_w_out, w_out, ffn2_norm, ffn2_w_gate, ffn2_w_up, ffn2_w_down, pe_norm, pe_w_gate, pe_w_proj, final_norm):
    depth = ffn1_norm.shape[0]
    bp, sp, _ = x_prompt.shape
    bs, ss, _ = x_sample.shape
    n_p, n_s = bp * sp, bs * ss
    tok = (x_prompt.reshape(n_p, D_MODEL), x_sample.reshape(n_s, D_MODEL))
    pp_all = p_prompt.reshape(depth, n_p, D_PLE)
    ps_all = p_sample.reshape(depth, n_s, D_PLE)
    conv_zero = jnp.zeros((1, bp, CONV_HIST, WA), F32)
    pool_zero = jnp.zeros((1, bp, POOL_HIST, WC), F32)
    ffn1_mats = (ffn1_w_gate, ffn1_w_up, ffn1_w_down)
    ffn1_vecs = (ffn1_norm,)
    mixer_mats = (w_in, conv_w_out, sgu_w_out, pool_w_out, w_out)
    mixer_vecs = (
        mix_norm, gate_b,
        conv_dw_w, conv_dw_b, conv_ln_g, conv_ln_b,
        sgu_ln_g, sgu_ln_b, sgu_w_s, jnp.swapaxes(sgu_b_s, 1, 2),
        pool_w, pool_scale,
    )
    ffn2_mats = (ffn2_w_gate, ffn2_w_up, ffn2_w_down, pe_w_gate, pe_w_proj)
    ffn2_vecs = (ffn2_norm, pe_norm, final_norm.reshape(1, -1))
    conv_p, conv_s, pool_p, pool_s, v_p, v_s = [], [], [], [], [], []
    for i in range(depth):
        last = i == depth - 1
        h_all = _ffn_call(tok, n_p, (), ffn1_mats, ffn1_vecs, layer=i, split_out=False, final=False)
        h_all, (cp, pp, vp), (cs, ps, vs) = _mixer_call(h_all, n_p, sp, ss, (conv_zero, pool_zero),
                                                        (cache_conv, cache_pool), mixer_mats, mixer_vecs, layer=i)
        out = _ffn_call((h_all,), n_p, (pp_all, ps_all), ffn2_mats, ffn2_vecs, layer=i, split_out=last,
                        final=last)
        tok = out if last else (out,)

        conv_p.append(cp); conv_s.append(cs); pool_p.append(pp); pool_s.append(ps)
        v_p.append(vp); v_s.append(vs)
    y_prompt, y_sample = tok
    return (y_prompt.reshape(bp, sp, D_MODEL), y_sample.reshape(bs, ss, D_MODEL), jnp.stack(conv_p),
            jnp.stack(conv_s), jnp.stack(pool_p), jnp.stack(pool_s), jnp.stack(v_p), jnp.stack(v_s))
```

```python
import functools

import jax
import jax.numpy as jnp
from jax import lax
from jax.experimental import pallas as pl
from jax.experimental.pallas import tpu as pltpu

D_MODEL = 1024
D_PLE = 256
D_FF = 2816
WA = 512
CONV_W = 31
CONV_HIST = CONV_W - 1
WB = 512
SGU_GROUPS = 4
SGU_CHUNK = 128
SGU_GC = WB // SGU_GROUPS
WC = 512
POOL_WINDOWS = (2, 4, 8, 16)
POOL_GC = WC // len(POOL_WINDOWS)
POOL_HIST = max(POOL_WINDOWS) - 1
D_IN = 2 * WA + 2 * WB + WC + 3 * D_MODEL
PAST_LEN = 2048
EPS = 1e-6

F32 = jnp.float32
BF16 = jnp.bfloat16

V7X_VMEM_LIMIT_BYTES = 60 * 1024 * 1024
LANES = 128
CONV_PAD = 32
POOL_PAD = 16
ROW_BLOCK = 32
GATE_COLS = 512
FF_CHUNKS = ((0, 1024), (1024, 1024), (2048, 768))
WEIGHT_STEPS = 8
N_MIXER_MATS = 5
N_MIXER_VECS = 12

PROMPT_TILE = 256
MIXER_SUB_TILES = 2
SAMPLE_STREAMS_PER_TILE = 4
FFN_TILE = 1024
FFN_PE_TILE = 512
FFN_SUB_TILE = 256


def _rms(x, g):
    return x * lax.rsqrt(jnp.mean(x * x, axis=-1, keepdims=True) + EPS) * g


def _ln(x, g, b):
    mu = jnp.mean(x, axis=-1, keepdims=True)
    xc = x - mu
    return xc * lax.rsqrt(jnp.mean(xc * xc, axis=-1, keepdims=True) + EPS) * g + b


def _dot(a, w):
    return jnp.dot(a, w, preferred_element_type=F32)


def _convert_chunk(step, chunk_ref, dst_ref):
    rows = chunk_ref.shape[0]
    r0 = pl.multiple_of(step * rows, rows)
    dst_ref[pl.ds(r0, rows), :] = chunk_ref[...].astype(BF16)


def _layer_view(ref, layer):
    return ref.at[pl.ds(layer, 1)] if len(ref.shape) == 2 else ref.at[layer]


def _chunk_spec(w, layer):
    rows = w.shape[1] // WEIGHT_STEPS
    return pl.BlockSpec((None, rows, w.shape[2]), lambda i: (layer, jnp.minimum(i, WEIGHT_STEPS - 1), 0))


def _half_swiglu(x, norm_ref, wg_ref, wu_ref, wd_ref):
    n = _rms(x, norm_ref[...]).astype(BF16)
    acc = None
    for c0, cw in FF_CHUNKS:
        g = _dot(n, wg_ref[:, c0:c0 + cw])
        u = _dot(n, wu_ref[:, c0:c0 + cw])
        d = _dot((g * jax.nn.sigmoid(g) * u).astype(BF16), wd_ref[c0:c0 + cw, :])
        acc = d if acc is None else acc + d
    return x + 0.5 * acc


def _mixer_kernel(*refs, layer, n_prompt_tiles, tiles_per_stream, sample_len):
    h_ref, hist_p, hist_s = refs[0], refs[1:3], refs[3:5]
    rest = refs[5:]
    chunks, rest = rest[:N_MIXER_MATS], rest[N_MIXER_MATS:]
    vecs, rest = tuple(_layer_view(v, layer) for v in rest[:N_MIXER_VECS]), rest[N_MIXER_VECS:]
    h_out, state_p, state_s, rest = rest[0], rest[1:4], rest[4:7], rest[7:]
    mats, bufs_p, bufs_s = rest[:N_MIXER_MATS], rest[N_MIXER_MATS:N_MIXER_MATS + 2], rest[N_MIXER_MATS + 2:]
    step = pl.program_id(0)

    def run(tile, hist, state, bufs, *, nseg, seg_len, carry, start):
        T = nseg * seg_len
        conv_out, pool_out, vn_out = state
        for sub in range(MIXER_SUB_TILES):
            rows = pl.ds(sub * T, T)
            segs = pl.ds(0 if carry else sub * nseg, nseg)
            _mixer_tile(tile * MIXER_SUB_TILES + sub, h_ref.at[rows], hist[0].at[segs], hist[1].at[segs],
                        *mats, *vecs, h_out.at[rows], conv_out.at[segs], pool_out.at[segs],
                        vn_out if carry else vn_out.at[segs], bufs[0].at[segs], bufs[1].at[segs],
                        nseg=nseg, seg_len=seg_len, carry=carry, start=start)

    @pl.when(step < WEIGHT_STEPS)
    def _():
        for chunk_ref, dst_ref in zip(chunks, mats):
            _convert_chunk(step, chunk_ref, dst_ref)

    @pl.when(jnp.logical_and(step >= WEIGHT_STEPS, step < WEIGHT_STEPS + n_prompt_tiles))
    def _():
        run(lax.rem(step - WEIGHT_STEPS, tiles_per_stream), hist_p, state_p, bufs_p,
            nseg=1, seg_len=PROMPT_TILE, carry=True, start=0)

    @pl.when(step >= WEIGHT_STEPS + n_prompt_tiles)
    def _():
        run(0, hist_s, state_s, bufs_s,
            nseg=SAMPLE_STREAMS_PER_TILE, seg_len=sample_len, carry=False, start=PAST_LEN)


def _mixer_tile(tile, h_ref, hconv_ref, hpool_ref,
                w_in, conv_wo, sgu_wo, pool_wo, w_out,
                mix_norm, gate_b, dw_w, dw_b, cln_g, cln_b, sln_g, sln_b, sgu_ws, sgu_bst, pool_w, pool_scale,
                h_out, conv_out, pool_out, vn_out, cbuf, pbuf, *, nseg, seg_len, carry, start):
    T = nseg * seg_len
    L = seg_len
    lane_groups = [slice(g * LANES, (g + 1) * LANES) for g in range(WA // LANES)]

    h = h_ref[...]
    n = _rms(h, mix_norm[...]).astype(BF16)

    def zin(c0, cw):
        return _dot(n, w_in[:, c0:c0 + cw])

    def load_history(s):
        for g, lanes in enumerate(lane_groups):
            cbuf[s, g, CONV_PAD - CONV_HIST:CONV_PAD, :] = hconv_ref[s, :, lanes]
            pbuf[s, g, POOL_PAD - POOL_HIST:POOL_PAD, :] = hpool_ref[s, :, lanes]

    if carry:
        pl.when(tile == 0)(functools.partial(load_history, 0))
    else:
        for s in range(nseg):
            load_history(s)

    glu = zin(0, WA) * jax.nn.sigmoid(zin(WA, WA))
    for s in range(nseg):
        for g, lanes in enumerate(lane_groups):
            cbuf[s, g, CONV_PAD:CONV_PAD + L, :] = glu[s * L:(s + 1) * L, lanes]

    def conv_block(s, r0):
        parts = []
        for g, lanes in enumerate(lane_groups):
            acc = None
            for k in range(CONV_W):
                off = CONV_PAD - CONV_HIST + k + r0
                term = dw_w[k:k + 1, lanes] * cbuf[s, g, off:off + ROW_BLOCK, :]
                acc = term if acc is None else acc + term
            parts.append(acc + dw_b[:, lanes])
        return jnp.concatenate(parts, axis=1)

    c_off = 2 * WA + 2 * WB
    g_off = c_off + WC
    gb = gate_b[...]

    def gate_part(j):
        cols = slice(j * GATE_COLS, (j + 1) * GATE_COLS)
        return jax.nn.sigmoid(zin(g_off + j * GATE_COLS, GATE_COLS) + gb[:, cols])

    mxu_tasks = ([functools.partial(zin, c_off, WC), functools.partial(zin, 2 * WA + WB, WB),
                  functools.partial(zin, 2 * WA, WB)]
                 + [functools.partial(gate_part, j) for j in range(3 * D_MODEL // GATE_COLS)])
    blocks = [(s, r0) for s in range(nseg) for r0 in range(0, L, ROW_BLOCK)]
    conv_rows, done = [], []
    for idx, (s, r0) in enumerate(blocks):
        conv_rows.append(conv_block(s, r0))
        while len(done) < (idx + 1) * len(mxu_tasks) // len(blocks):
            done.append(mxu_tasks[len(done)]())
    c, v, u = done[0], done[1], done[2]
    parts_per_gate = D_MODEL // GATE_COLS
    gates = [jnp.concatenate(done[3 + b * parts_per_gate:3 + (b + 1) * parts_per_gate], axis=1)
             for b in range(3)]

    for s in range(nseg):
        for g, lanes in enumerate(lane_groups):
            conv_out[s, :, lanes] = cbuf[s, g, L + CONV_PAD - CONV_HIST:L + CONV_PAD, :]
    if carry:
        for g in range(len(lane_groups)):
            cbuf[0, g, 0:CONV_PAD, :] = cbuf[0, g, L:L + CONV_PAD, :]

    ya = _ln(jnp.concatenate(conv_rows, axis=0), cln_g[...], cln_b[...])
    m = gates[0] * _dot((ya * jax.nn.sigmoid(ya)).astype(BF16), conv_wo[...])

    vn = _ln(v, sln_g[...], sln_b[...])
    if carry:
        vn_out[...] = vn[T - SGU_CHUNK:T]
    else:
        for s in range(nseg):
            vn_out[s] = vn[s * L:(s + 1) * L]
    rows = min(L, SGU_CHUNK)
    ri = lax.broadcasted_iota(jnp.int32, (rows, rows), 0)
    ci = lax.broadcasted_iota(jnp.int32, (rows, rows), 1)
    ws_tril = [jnp.where(ci <= ri, sgu_ws[g, 0:rows, 0:rows], 0.0) for g in range(SGU_GROUPS)]
    bst = sgu_bst[...]
    s_rows = []
    for r0 in range(0, T, rows):
        parts = []
        for g in range(SGU_GROUPS):
            sg = _dot(ws_tril[g], vn[r0:r0 + rows, g * SGU_GC:(g + 1) * SGU_GC])
            parts.append(sg + bst[0:rows, g:g + 1])
        s_rows.append(jnp.concatenate(parts, axis=1))
    m = m + gates[1] * _dot((u * jnp.concatenate(s_rows, axis=0)).astype(BF16), sgu_wo[...])

    for s in range(nseg):
        for g, lanes in enumerate(lane_groups):
            pbuf[s, g, POOL_PAD:POOL_PAD + L, :] = c[s * L:(s + 1) * L, lanes]
    pos0 = start + (tile * L if carry else 0)
    pos = pos0 + lax.broadcasted_iota(jnp.int32, (L, 1), 0)
    inv_cnt = [1.0 / jnp.minimum(pos + 1, w).astype(F32) for w in POOL_WINDOWS]
    d_rows = []
    for s in range(nseg):
        for r0 in range(0, L, ROW_BLOCK):
            parts = []
            for g, w in enumerate(POOL_WINDOWS):
                tot = None
                for j in range(w):
                    lo = POOL_PAD - j + r0
                    term = pbuf[s, g, lo:lo + ROW_BLOCK, :]
                    tot = term if tot is None else tot + term
                cur = pbuf[s, g, POOL_PAD + r0:POOL_PAD + r0 + ROW_BLOCK, :]
                parts.append(tot * inv_cnt[g][r0:r0 + ROW_BLOCK] - cur)
            d_rows.append(jnp.concatenate(parts, axis=1))
        for g, lanes in enumerate(lane_groups):
            pool_out[s, :, lanes] = pbuf[s, g, L + POOL_PAD - POOL_HIST:L + POOL_PAD, :]
    if carry:
        for g in range(len(lane_groups)):
            pbuf[0, g, 0:POOL_PAD, :] = pbuf[0, g, L:L + POOL_PAD, :]
    dd = jnp.concatenate(d_rows, axis=0)
    yc = jnp.concatenate(
        [_dot(dd[:, g * POOL_GC:(g + 1) * POOL_GC], pool_w[g]) for g in range(len(POOL_WINDOWS))], axis=1)
    m = m + gates[2] * _dot((yc * pool_scale[...]).astype(BF16), pool_wo[...])

    h_out[...] = h + _dot(m.astype(BF16), w_out[...])


def _ffn_kernel(*refs, layer, n_prompt_tiles, split_in, split_out, with_pe, final):
    n_mats = 5 if with_pe else 3
    n_in = 2 if split_in else 1
    n_out = 2 if split_out else 1
    tok_in, rest = refs[:n_in], refs[n_in:]
    if with_pe:
        (pp_ref, ps_ref), rest = rest[:2], rest[2:]
    chunks, rest = rest[:n_mats], rest[n_mats:]
    mats = rest[-n_mats:]
    vecs, tok_out = rest[:-n_mats - n_out], rest[-n_mats - n_out:-n_mats]
    if with_pe:
        f_norm, pe_norm, fin_norm = vecs
        f_wg, f_wu, f_wd, pe_wg, pe_wp = mats
        pe_norm = _layer_view(pe_norm, layer)
    else:
        f_norm, = vecs
        f_wg, f_wu, f_wd = mats
    f_norm = _layer_view(f_norm, layer)
    step = pl.program_id(0)

    @pl.when(step < WEIGHT_STEPS)
    def _():
        for chunk_ref, dst_ref in zip(chunks, mats):
            _convert_chunk(step, chunk_ref, dst_ref)

    @pl.when(step >= WEIGHT_STEPS)
    def _():
        is_prompt = step < WEIGHT_STEPS + n_prompt_tiles

        def pick(prompt_ref, sample_ref, rs):
            return jnp.where(is_prompt, prompt_ref[rs, :], sample_ref[rs, :])

        sub_tiles = [slice(r0, r0 + FFN_SUB_TILE) for r0 in range(0, tok_out[0].shape[0], FFN_SUB_TILE)]
        results = [_half_swiglu(pick(*tok_in, rs) if split_in else tok_in[0][rs, :], f_norm, f_wg, f_wu, f_wd)
                   for rs in sub_tiles]
        if with_pe:
            gated = []
            for rs, h in zip(sub_tiles, results):
                gate = jax.nn.sigmoid(_dot(_rms(h, pe_norm[...]).astype(BF16), pe_wg[...]))
                gated.append(h + gate * _dot(pick(pp_ref, ps_ref, rs).astype(BF16), pe_wp[...]))
            results = gated
        if final:
            results = [_rms(h, fin_norm[...]) for h in results]

        def store(o_ref):
            for rs, h in zip(sub_tiles, results):
                o_ref[rs, :] = h

        if split_out:
            pl.when(is_prompt)(functools.partial(store, tok_out[0]))
            pl.when(jnp.logical_not(is_prompt))(functools.partial(store, tok_out[1]))
        else:
            store(tok_out[0])


def _resident(arr):
    del arr
    return pl.BlockSpec(memory_space=pltpu.VMEM)


def _mixer_call(x, n_prompt_rows, prompt_len, sample_len, hist_p, hist_s, mats, vecs, *, layer):
    assert len(mats) == N_MIXER_MATS and len(vecs) == N_MIXER_VECS
    n_rows = x.shape[0]
    T = MIXER_SUB_TILES * PROMPT_TILE
    sseg = MIXER_SUB_TILES * SAMPLE_STREAMS_PER_TILE
    assert sseg * sample_len == T
    tiles_per_stream = prompt_len // T
    n_p_tiles = n_prompt_rows // T
    n_s_tiles = (n_rows - n_prompt_rows) // T
    n_p_streams, n_s_streams = hist_p[0].shape[1], hist_s[0].shape[1]

    def tile_idx(i):
        return jnp.clip(i - WEIGHT_STEPS, 0, n_p_tiles + n_s_tiles - 1)

    def p_idx(i):
        return jnp.minimum(tile_idx(i) // tiles_per_stream, n_p_streams - 1), 0, 0

    def s_idx(i):
        return jnp.maximum(tile_idx(i) - n_p_tiles, 0), 0, 0

    tok_spec = pl.BlockSpec((T, D_MODEL), lambda i: (tile_idx(i), 0))
    in_specs = [
        tok_spec,
        pl.BlockSpec((None, 1, CONV_HIST, WA), lambda i: (0,) + p_idx(i)),
        pl.BlockSpec((None, 1, POOL_HIST, WC), lambda i: (0,) + p_idx(i)),
        pl.BlockSpec((None, sseg, CONV_HIST, WA), lambda i: (layer,) + s_idx(i)),
        pl.BlockSpec((None, sseg, POOL_HIST, WC), lambda i: (layer,) + s_idx(i)),
    ] + [_chunk_spec(w, layer) for w in mats] + [_resident(w) for w in vecs]
    out_shape = (
        jax.ShapeDtypeStruct((n_rows, D_MODEL), F32),
        jax.ShapeDtypeStruct((n_p_streams, CONV_HIST, WA), F32),
        jax.ShapeDtypeStruct((n_p_streams, POOL_HIST, WC), F32),
        jax.ShapeDtypeStruct((n_p_streams, SGU_CHUNK, WB), F32),
        jax.ShapeDtypeStruct((n_s_streams, CONV_HIST, WA), F32),
        jax.ShapeDtypeStruct((n_s_streams, POOL_HIST, WC), F32),
        jax.ShapeDtypeStruct((n_s_streams, sample_len, WB), F32),
    )
    out_specs = (
        tok_spec,
        pl.BlockSpec((1, CONV_HIST, WA), p_idx),
        pl.BlockSpec((1, POOL_HIST, WC), p_idx),
        pl.BlockSpec((None, SGU_CHUNK, WB), p_idx),
        pl.BlockSpec((sseg, CONV_HIST, WA), s_idx),
        pl.BlockSpec((sseg, POOL_HIST, WC), s_idx),
        pl.BlockSpec((sseg, sample_len, WB), s_idx),
    )
    out = pl.pallas_call(
        functools.partial(_mixer_kernel, layer=layer, n_prompt_tiles=n_p_tiles, tiles_per_stream=tiles_per_stream,
                          sample_len=sample_len),
        grid=(WEIGHT_STEPS + n_p_tiles + n_s_tiles,),
        in_specs=in_specs,
        out_specs=out_specs,
        out_shape=out_shape,
        scratch_shapes=[pltpu.VMEM(w.shape[1:], BF16) for w in mats] + [
            pltpu.VMEM((1, WA // LANES, CONV_PAD + PROMPT_TILE, LANES), F32),
            pltpu.VMEM((1, WC // LANES, POOL_PAD + PROMPT_TILE, LANES), F32),
            pltpu.VMEM((sseg, WA // LANES, CONV_PAD + sample_len, LANES), F32),
            pltpu.VMEM((sseg, WC // LANES, POOL_PAD + sample_len, LANES), F32),
        ],
        compiler_params=pltpu.CompilerParams(
            dimension_semantics=("arbitrary",),
            vmem_limit_bytes=V7X_VMEM_LIMIT_BYTES),
        name="mixer",
    )(x, *hist_p, *hist_s, *mats, *vecs)
    return out[0], out[1:4], out[4:7]


def _ffn_call(tok, n_prompt_rows, pe_inputs, big_w, wts, *, layer, split_out, final):
    split_in = len(tok) == 2
    with_pe = bool(pe_inputs)
    tile = FFN_PE_TILE if with_pe else FFN_TILE
    n_rows = sum(t.shape[0] for t in tok)
    n_p = n_prompt_rows // tile
    n_s = (n_rows - n_prompt_rows) // tile
    prompt_idx = lambda i: (jnp.clip(i - WEIGHT_STEPS, 0, n_p - 1), 0)
    sample_idx = lambda i: (jnp.clip(i - WEIGHT_STEPS - n_p, 0, n_s - 1), 0)
    all_idx = lambda i: (jnp.clip(i - WEIGHT_STEPS, 0, n_p + n_s - 1), 0)
    pair_specs = [pl.BlockSpec((tile, D_MODEL), prompt_idx), pl.BlockSpec((tile, D_MODEL), sample_idx)]
    one_spec = [pl.BlockSpec((tile, D_MODEL), all_idx)]
    pe_specs = [pl.BlockSpec((None, tile, D_PLE), lambda i: (layer,) + prompt_idx(i)),
                pl.BlockSpec((None, tile, D_PLE), lambda i: (layer,) + sample_idx(i))]
    if split_out:
        out_shape = (jax.ShapeDtypeStruct((n_prompt_rows, D_MODEL), F32),
                     jax.ShapeDtypeStruct((n_rows - n_prompt_rows, D_MODEL), F32))
    else:
        out_shape = (jax.ShapeDtypeStruct((n_rows, D_MODEL), F32),)
    out = pl.pallas_call(
        functools.partial(_ffn_kernel, layer=layer, n_prompt_tiles=n_p, split_in=split_in, split_out=split_out,
                          with_pe=with_pe, final=final),
        grid=(WEIGHT_STEPS + n_p + n_s,),
        in_specs=((pair_specs if split_in else one_spec) + (pe_specs if with_pe else [])
                  + [_chunk_spec(w, layer) for w in big_w] + [_resident(w) for w in wts]),
        out_specs=tuple(pair_specs if split_out else one_spec),
        out_shape=out_shape,
        scratch_shapes=[pltpu.VMEM(w.shape[1:], BF16) for w in big_w],
        compiler_params=pltpu.CompilerParams(
            dimension_semantics=("arbitrary",),
            vmem_limit_bytes=V7X_VMEM_LIMIT_BYTES),
        name="ffn_pe" if with_pe else "ffn",
    )(*tok, *pe_inputs, *big_w, *wts)
    return out if split_out else out[0]


def kernel(x_prompt, x_sample, p_prompt, p_sample, cache_conv, cache_pool, ffn1_norm, ffn1_w_gate, ffn1_w_up, ffn1_w_down, mix_norm, w_in, gate_b, conv_dw_w, conv_dw_b, conv_ln_g, conv_ln_b, conv_w_out, sgu_ln_g, sgu_ln_b, sgu_w_s, sgu_b_s, sgu_w_out, pool_w, pool_scale, pool_w_out, w_out, ffn2_norm, ffn2_w_gate, ffn2_w_up, ffn2_w_down, pe_norm, pe_w_gate, pe_w_proj, final_norm):
    depth = ffn1_norm.shape[0]
    bp, sp, _ = x_prompt.shape
    bs, ss, _ = x_sample.shape
    n_p, n_s = bp * sp, bs * ss
    tok = (x_prompt.reshape(n_p, D_MODEL), x_sample.reshape(n_s, D_MODEL))
    pp_all = p_prompt.reshape(depth, n_p, D_PLE)
    ps_all = p_sample.reshape(depth, n_s, D_PLE)
    conv_zero = jnp.zeros((1, bp, CONV_HIST, WA), F32)
    pool_zero = jnp.zeros((1, bp, POOL_HIST, WC), F32)
    ffn1_mats = (ffn1_w_gate, ffn1_w_up, ffn1_w_down)
    ffn1_vecs = (ffn1_norm,)
    mixer_mats = (w_in, conv_w_out, sgu_w_out, pool_w_out, w_out)
    mixer_vecs = (
        mix_norm, gate_b,
        conv_dw_w, conv_dw_b, conv_ln_g, conv_ln_b,
        sgu_ln_g, sgu_ln_b, sgu_w_s, jnp.swapaxes(sgu_b_s, 1, 2),
        pool_w, pool_scale,
    )
    ffn2_mats = (ffn2_w_gate, ffn2_w_up, ffn2_w_down, pe_w_gate, pe_w_proj)
    ffn2_vecs = (ffn2_norm, pe_norm, final_norm.reshape(1, -1))
    conv_p, conv_s, pool_p, pool_s, v_p, v_s = [], [], [], [], [], []
    for i in range(depth):
        last = i == depth - 1
        h_all = _ffn_call(tok, n_p, (), ffn1_mats, ffn1_vecs, layer=i, split_out=False, final=False)
        h_all, (cp, pp, vp), (cs, ps, vs) = _mixer_call(h_all, n_p, sp, ss, (conv_zero, pool_zero),
                                                        (cache_conv, cache_pool), mixer_mats, mixer_vecs, layer=i)
        out = _ffn_call((h_all,), n_p, (pp_all, ps_all), ffn2_mats, ffn2_vecs, layer=i, split_out=last,
                        final=last)
        tok = out if last else (out,)

        conv_p.append(cp); conv_s.append(cs); pool_p.append(pp); pool_s.append(ps)
        v_p.append(vp); v_s.append(vs)
    y_prompt, y_sample = tok
    return (y_prompt.reshape(bp, sp, D_MODEL), y_sample.reshape(bs, ss, D_MODEL), jnp.stack(conv_p),
            jnp.stack(conv_s), jnp.stack(pool_p), jnp.stack(pool_s), jnp.stack(v_p), jnp.stack(v_s))
```

```python
import functools

import jax
import jax.numpy as jnp
from jax import lax
from jax.experimental import pallas as pl
from jax.experimental.pallas import tpu as pltpu

D_MODEL = 1024
D_PLE = 256
D_FF = 2816
WA = 512
CONV_W = 31
CONV_HIST = CONV_W - 1
WB = 512
SGU_GROUPS = 4
SGU_CHUNK = 128
SGU_GC = WB // SGU_GROUPS
WC = 512
POOL_WINDOWS = (2, 4, 8, 16)
POOL_GC = WC // len(POOL_WINDOWS)
POOL_HIST = max(POOL_WINDOWS) - 1
D_IN = 2 * WA + 2 * WB + WC + 3 * D_MODEL
PAST_LEN = 2048
EPS = 1e-6

F32 = jnp.float32
BF16 = jnp.bfloat16

V7X_VMEM_LIMIT_BYTES = 60 * 1024 * 1024
LANES = 128
CONV_PAD = 32
POOL_PAD = 16
ROW_BLOCK = 32
GATE_COLS = 512
FF_CHUNKS = ((0, 1024), (1024, 1024), (2048, 768))
WEIGHT_STEPS = 8
N_MIXER_MATS = 5
N_MIXER_VECS = 12

PROMPT_TILE = 256
MIXER_SUB_TILES = 2
SAMPLE_STREAMS_PER_TILE = 4
FFN_TILE = 1024
FFN_PE_TILE = 512
FFN_SUB_TILE = 256


def _rms(x, g):
    return x * lax.rsqrt(jnp.mean(x * x, axis=-1, keepdims=True) + EPS) * g


def _ln(x, g, b):
    mu = jnp.mean(x, axis=-1, keepdims=True)
    xc = x - mu
    return xc * lax.rsqrt(jnp.mean(xc * xc, axis=-1, keepdims=True) + EPS) * g + b


def _dot(a, w):
    return jnp.dot(a, w, preferred_element_type=F32)


def _convert_chunk(step, chunk_ref, dst_ref):
    rows = chunk_ref.shape[0]
    r0 = pl.multiple_of(step * rows, rows)
    dst_ref[pl.ds(r0, rows), :] = chunk_ref[...].astype(BF16)


def _layer_view(ref, layer):
    return ref.at[pl.ds(layer, 1)] if len(ref.shape) == 2 else ref.at[layer]


def _chunk_spec(w, layer):
    rows = w.shape[1] // WEIGHT_STEPS
    return pl.BlockSpec((None, rows, w.shape[2]), lambda i: (layer, jnp.minimum(i, WEIGHT_STEPS - 1), 0))


def _half_swiglu(x, norm_ref, wg_ref, wu_ref, wd_ref):
    n = _rms(x, norm_ref[...]).astype(BF16)
    acc = None
    for c0, cw in FF_CHUNKS:
        g = _dot(n, wg_ref[:, c0:c0 + cw])
        u = _dot(n, wu_ref[:, c0:c0 + cw])
        d = _dot((g * jax.nn.sigmoid(g) * u).astype(BF16), wd_ref[c0:c0 + cw, :])
        acc = d if acc is None else acc + d
    return x + 0.5 * acc


def _mixer_kernel(*refs, layer, n_prompt_tiles, tiles_per_stream, sample_len):
    h_ref, hist_p, hist_s = refs[0], refs[1:3], refs[3:5]
    rest = refs[5:]
    chunks, rest = rest[:N_MIXER_MATS], rest[N_MIXER_MATS:]
    vecs, rest = tuple(_layer_view(v, layer) for v in rest[:N_MIXER_VECS]), rest[N_MIXER_VECS:]
    h_out, state_p, state_s, rest = rest[0], rest[1:4], rest[4:7], rest[7:]
    mats, bufs_p, bufs_s = rest[:N_MIXER_MATS], rest[N_MIXER_MATS:N_MIXER_MATS + 2], rest[N_MIXER_MATS + 2:]
    step = pl.program_id(0)

    def run(tile, hist, state, bufs, *, nseg, seg_len, carry, start):
        T = nseg * seg_len
        conv_out, pool_out, vn_out = state
        for sub in range(MIXER_SUB_TILES):
            rows = pl.ds(sub * T, T)
            segs = pl.ds(0 if carry else sub * nseg, nseg)
            _mixer_tile(tile * MIXER_SUB_TILES + sub, h_ref.at[rows], hist[0].at[segs], hist[1].at[segs],
                        *mats, *vecs, h_out.at[rows], conv_out.at[segs], pool_out.at[segs],
                        vn_out if carry else vn_out.at[segs], bufs[0].at[segs], bufs[1].at[segs],
                        nseg=nseg, seg_len=seg_len, carry=carry, start=start)

    @pl.when(step < WEIGHT_STEPS)
    def _():
        for chunk_ref, dst_ref in zip(chunks, mats):
            _convert_chunk(step, chunk_ref, dst_ref)

    @pl.when(jnp.logical_and(step >= WEIGHT_STEPS, step < WEIGHT_STEPS + n_prompt_tiles))
    def _():
        run(lax.rem(step - WEIGHT_STEPS, tiles_per_stream), hist_p, state_p, bufs_p,
            nseg=1, seg_len=PROMPT_TILE, carry=True, start=0)

    @pl.when(step >= WEIGHT_STEPS + n_prompt_tiles)
    def _():
        run(0, hist_s, state_s, bufs_s,
            nseg=SAMPLE_STREAMS_PER_TILE, seg_len=sample_len, carry=False, start=PAST_LEN)


def _mixer_tile(tile, h_ref, hconv_ref, hpool_ref,
                w_in, conv_wo, sgu_wo, pool_wo, w_out,
                mix_norm, gate_b, dw_w, dw_b, cln_g, cln_b, sln_g, sln_b, sgu_ws, sgu_bst, pool_w, pool_scale,
                h_out, conv_out, pool_out, vn_out, cbuf, pbuf, *, nseg, seg_len, carry, start):
    T = nseg * seg_len
    L = seg_len
    lane_groups = [slice(g * LANES, (g + 1) * LANES) for g in range(WA // LANES)]

    h = h_ref[...]
    n = _rms(h, mix_norm[...]).astype(BF16)

    def zin(c0, cw):
        return _dot(n, w_in[:, c0:c0 + cw])

    def load_history(s):
        for g, lanes in enumerate(lane_groups):
            cbuf[s, g, CONV_PAD - CONV_HIST:CONV_PAD, :] = hconv_ref[s, :, lanes]
            pbuf[s, g, POOL_PAD - POOL_HIST:POOL_PAD, :] = hpool_ref[s, :, lanes]

    if carry:
        pl.when(tile == 0)(functools.partial(load_history, 0))
    else:
        for s in range(nseg):
            load_history(s)

    glu = zin(0, WA) * jax.nn.sigmoid(zin(WA, WA))
    for s in range(nseg):
        for g, lanes in enumerate(lane_groups):
            cbuf[s, g, CONV_PAD:CONV_PAD + L, :] = glu[s * L:(s + 1) * L, lanes]

    def conv_group(g, lanes):
        taps = [jnp.broadcast_to(dw_w[k:k + 1, lanes], (ROW_BLOCK, LANES)) for k in range(CONV_W)]
        bias = dw_b[:, lanes]
        out = []
        for s, r0 in blocks:
            acc = None
            for k in range(CONV_W):
                off = CONV_PAD - CONV_HIST + k + r0
                term = taps[k] * cbuf[s, g, off:off + ROW_BLOCK, :]
                acc = term if acc is None else acc + term
            out.append(acc + bias)
        return out

    blocks = [(s, r0) for s in range(nseg) for r0 in range(0, L, ROW_BLOCK)]
    c_off = 2 * WA + 2 * WB
    g_off = c_off + WC
    gb = gate_b[...]

    def gate_part(j):
        cols = slice(j * GATE_COLS, (j + 1) * GATE_COLS)
        return jax.nn.sigmoid(zin(g_off + j * GATE_COLS, GATE_COLS) + gb[:, cols])

    mxu_tasks = ([functools.partial(zin, c_off, WC), functools.partial(zin, 2 * WA + WB, WB),
                  functools.partial(zin, 2 * WA, WB)]
                 + [functools.partial(gate_part, j) for j in range(3 * D_MODEL // GATE_COLS)])
    conv_cols, done = [], []
    for g, lanes in enumerate(lane_groups):
        conv_cols.append(conv_group(g, lanes))
        while len(done) < (g + 1) * len(mxu_tasks) // len(lane_groups):
            done.append(mxu_tasks[len(done)]())
    conv_rows = [jnp.concatenate([col[b] for col in conv_cols], axis=1) for b in range(len(blocks))]
    c, v, u = done[0], done[1], done[2]
    parts_per_gate = D_MODEL // GATE_COLS
    gates = [jnp.concatenate(done[3 + b * parts_per_gate:3 + (b + 1) * parts_per_gate], axis=1)
             for b in range(3)]

    for s in range(nseg):
        for g, lanes in enumerate(lane_groups):
            conv_out[s, :, lanes] = cbuf[s, g, L + CONV_PAD - CONV_HIST:L + CONV_PAD, :]
    if carry:
        for g in range(len(lane_groups)):
            cbuf[0, g, 0:CONV_PAD, :] = cbuf[0, g, L:L + CONV_PAD, :]

    ya = _ln(jnp.concatenate(conv_rows, axis=0), cln_g[...], cln_b[...])
    m = gates[0] * _dot((ya * jax.nn.sigmoid(ya)).astype(BF16), conv_wo[...])

    vn = _ln(v, sln_g[...], sln_b[...])
    if carry:
        vn_out[...] = vn[T - SGU_CHUNK:T]
    else:
        for s in range(nseg):
            vn_out[s] = vn[s * L:(s + 1) * L]
    rows = min(L, SGU_CHUNK)
    ri = lax.broadcasted_iota(jnp.int32, (rows, rows), 0)
    ci = lax.broadcasted_iota(jnp.int32, (rows, rows), 1)
    ws_tril = [jnp.where(ci <= ri, sgu_ws[g, 0:rows, 0:rows], 0.0) for g in range(SGU_GROUPS)]
    bst = sgu_bst[...]
    s_rows = []
    for r0 in range(0, T, rows):
        parts = []
        for g in range(SGU_GROUPS):
            sg = _dot(ws_tril[g], vn[r0:r0 + rows, g * SGU_GC:(g + 1) * SGU_GC])
            parts.append(sg + bst[0:rows, g:g + 1])
        s_rows.append(jnp.concatenate(parts, axis=1))
    m = m + gates[1] * _dot((u * jnp.concatenate(s_rows, axis=0)).astype(BF16), sgu_wo[...])

    for s in range(nseg):
        for g, lanes in enumerate(lane_groups):
            pbuf[s, g, POOL_PAD:POOL_PAD + L, :] = c[s * L:(s + 1) * L, lanes]
    pos0 = start + (tile * L if carry else 0)
    pos = pos0 + lax.broadcasted_iota(jnp.int32, (L, 1), 0)
    inv_cnt = [1.0 / jnp.minimum(pos + 1, w).astype(F32) for w in POOL_WINDOWS]
    d_rows = []
    for s in range(nseg):
        for r0 in range(0, L, ROW_BLOCK):
            parts = []
            for g, w in enumerate(POOL_WINDOWS):
                tot = None
                for j in range(w):
                    lo = POOL_PAD - j + r0
                    term = pbuf[s, g, lo:lo + ROW_BLOCK, :]
                    tot = term if tot is None else tot + term
                cur = pbuf[s, g, POOL_PAD + r0:POOL_PAD + r0 + ROW_BLOCK, :]
                parts.append(tot * inv_cnt[g][r0:r0 + ROW_BLOCK] - cur)
            d_rows.append(jnp.concatenate(parts, axis=1))
        for g, lanes in enumerate(lane_groups):
            pool_out[s, :, lanes] = pbuf[s, g, L + POOL_PAD - POOL_HIST:L + POOL_PAD, :]
    if carry:
        for g in range(len(lane_groups)):
            pbuf[0, g, 0:POOL_PAD, :] = pbuf[0, g, L:L + POOL_PAD, :]
    dd = jnp.concatenate(d_rows, axis=0)
    yc = jnp.concatenate(
        [_dot(dd[:, g * POOL_GC:(g + 1) * POOL_GC], pool_w[g]) for g in range(len(POOL_WINDOWS))], axis=1)
    m = m + gates[2] * _dot((yc * pool_scale[...]).astype(BF16), pool_wo[...])

    h_out[...] = h + _dot(m.astype(BF16), w_out[...])


def _ffn_kernel(*refs, layer, n_prompt_tiles, split_in, split_out, with_pe, final):
    n_mats = 5 if with_pe else 3
    n_in = 2 if split_in else 1
    n_out = 2 if split_out else 1
    tok_in, rest = refs[:n_in], refs[n_in:]
    if with_pe:
        (pp_ref, ps_ref), rest = rest[:2], rest[2:]
    chunks, rest = rest[:n_mats], rest[n_mats:]
    mats = rest[-n_mats:]
    vecs, tok_out = rest[:-n_mats - n_out], rest[-n_mats - n_out:-n_mats]
    if with_pe:
        f_norm, pe_norm, fin_norm = vecs
        f_wg, f_wu, f_wd, pe_wg, pe_wp = mats
        pe_norm = _layer_view(pe_norm, layer)
    else:
        f_norm, = vecs
        f_wg, f_wu, f_wd = mats
    f_norm = _layer_view(f_norm, layer)
    step = pl.program_id(0)

    @pl.when(step < WEIGHT_STEPS)
    def _():
        for chunk_ref, dst_ref in zip(chunks, mats):
            _convert_chunk(step, chunk_ref, dst_ref)

    @pl.when(step >= WEIGHT_STEPS)
    def _():
        is_prompt = step < WEIGHT_STEPS + n_prompt_tiles

        def pick(prompt_ref, sample_ref, rs):
            return jnp.where(is_prompt, prompt_ref[rs, :], sample_ref[rs, :])

        sub_tiles = [slice(r0, r0 + FFN_SUB_TILE) for r0 in range(0, tok_out[0].shape[0], FFN_SUB_TILE)]
        results = [_half_swiglu(pick(*tok_in, rs) if split_in else tok_in[0][rs, :], f_norm, f_wg, f_wu, f_wd)
                   for rs in sub_tiles]
        if with_pe:
            gated = []
            for rs, h in zip(sub_tiles, results):
                gate = jax.nn.sigmoid(_dot(_rms(h, pe_norm[...]).astype(BF16), pe_wg[...]))
                gated.append(h + gate * _dot(pick(pp_ref, ps_ref, rs).astype(BF16), pe_wp[...]))
            results = gated
        if final:
            results = [_rms(h, fin_norm[...]) for h in results]

        def store(o_ref):
            for rs, h in zip(sub_tiles, results):
                o_ref[rs, :] = h

        if split_out:
            pl.when(is_prompt)(functools.partial(store, tok_out[0]))
            pl.when(jnp.logical_not(is_prompt))(functools.partial(store, tok_out[1]))
        else:
            store(tok_out[0])


def _resident(arr):
    del arr
    return pl.BlockSpec(memory_space=pltpu.VMEM)


def _mixer_call(x, n_prompt_rows, prompt_len, sample_len, hist_p, hist_s, mats, vecs, *, layer):
    assert len(mats) == N_MIXER_MATS and len(vecs) == N_MIXER_VECS
    n_rows = x.shape[0]
    T = MIXER_SUB_TILES * PROMPT_TILE
    sseg = MIXER_SUB_TILES * SAMPLE_STREAMS_PER_TILE
    assert sseg * sample_len == T
    tiles_per_stream = prompt_len // T
    n_p_tiles = n_prompt_rows // T
    n_s_tiles = (n_rows - n_prompt_rows) // T
    n_p_streams, n_s_streams = hist_p[0].shape[1], hist_s[0].shape[1]

    def tile_idx(i):
        return jnp.clip(i - WEIGHT_STEPS, 0, n_p_tiles + n_s_tiles - 1)

    def p_idx(i):
        return jnp.minimum(tile_idx(i) // tiles_per_stream, n_p_streams - 1), 0, 0

    def s_idx(i):
        return jnp.maximum(tile_idx(i) - n_p_tiles, 0), 0, 0

    tok_spec = pl.BlockSpec((T, D_MODEL), lambda i: (tile_idx(i), 0))
    in_specs = [
        tok_spec,
        pl.BlockSpec((None, 1, CONV_HIST, WA), lambda i: (0,) + p_idx(i)),
        pl.BlockSpec((None, 1, POOL_HIST, WC), lambda i: (0,) + p_idx(i)),
        pl.BlockSpec((None, sseg, CONV_HIST, WA), lambda i: (layer,) + s_idx(i)),
        pl.BlockSpec((None, sseg, POOL_HIST, WC), lambda i: (layer,) + s_idx(i)),
    ] + [_chunk_spec(w, layer) for w in mats] + [_resident(w) for w in vecs]
    out_shape = (
        jax.ShapeDtypeStruct((n_rows, D_MODEL), F32),
        jax.ShapeDtypeStruct((n_p_streams, CONV_HIST, WA), F32),
        jax.ShapeDtypeStruct((n_p_streams, POOL_HIST, WC), F32),
        jax.ShapeDtypeStruct((n_p_streams, SGU_CHUNK, WB), F32),
        jax.ShapeDtypeStruct((n_s_streams, CONV_HIST, WA), F32),
        jax.ShapeDtypeStruct((n_s_streams, POOL_HIST, WC), F32),
        jax.ShapeDtypeStruct((n_s_streams, sample_len, WB), F32),
    )
    out_specs = (
        tok_spec,
        pl.BlockSpec((1, CONV_HIST, WA), p_idx),
        pl.BlockSpec((1, POOL_HIST, WC), p_idx),
        pl.BlockSpec((None, SGU_CHUNK, WB), p_idx),
        pl.BlockSpec((sseg, CONV_HIST, WA), s_idx),
        pl.BlockSpec((sseg, POOL_HIST, WC), s_idx),
        pl.BlockSpec((sseg, sample_len, WB), s_idx),
    )
    out = pl.pallas_call(
        functools.partial(_mixer_kernel, layer=layer, n_prompt_tiles=n_p_tiles, tiles_per_stream=tiles_per_stream,
                          sample_len=sample_len),
        grid=(WEIGHT_STEPS + n_p_tiles + n_s_tiles,),
        in_specs=in_specs,
        out_specs=out_specs,
        out_shape=out_shape,
        scratch_shapes=[pltpu.VMEM(w.shape[1:], BF16) for w in mats] + [
            pltpu.VMEM((1, WA // LANES, CONV_PAD + PROMPT_TILE, LANES), F32),
            pltpu.VMEM((1, WC // LANES, POOL_PAD + PROMPT_TILE, LANES), F32),
            pltpu.VMEM((sseg, WA // LANES, CONV_PAD + sample_len, LANES), F32),
            pltpu.VMEM((sseg, WC // LANES, POOL_PAD + sample_len, LANES), F32),
        ],
        compiler_params=pltpu.CompilerParams(
            dimension_semantics=("arbitrary",),
            vmem_limit_bytes=V7X_VMEM_LIMIT_BYTES),
        name="mixer",
    )(x, *hist_p, *hist_s, *mats, *vecs)
    return out[0], out[1:4], out[4:7]


def _ffn_call(tok, n_prompt_rows, pe_inputs, big_w, wts, *, layer, split_out, final):
    split_in = len(tok) == 2
    with_pe = bool(pe_inputs)
    tile = FFN_PE_TILE if with_pe else FFN_TILE
    n_rows = sum(t.shape[0] for t in tok)
    n_p = n_prompt_rows // tile
    n_s = (n_rows - n_prompt_rows) // tile
    prompt_idx = lambda i: (jnp.clip(i - WEIGHT_STEPS, 0, n_p - 1), 0)
    sample_idx = lambda i: (jnp.clip(i - WEIGHT_STEPS - n_p, 0, n_s - 1), 0)
    all_idx = lambda i: (jnp.clip(i - WEIGHT_STEPS, 0, n_p + n_s - 1), 0)
    pair_specs = [pl.BlockSpec((tile, D_MODEL), prompt_idx), pl.BlockSpec((tile, D_MODEL), sample_idx)]
    one_spec = [pl.BlockSpec((tile, D_MODEL), all_idx)]
    pe_specs = [pl.BlockSpec((None, tile, D_PLE), lambda i: (layer,) + prompt_idx(i)),
                pl.BlockSpec((None, tile, D_PLE), lambda i: (layer,) + sample_idx(i))]
    if split_out:
        out_shape = (jax.ShapeDtypeStruct((n_prompt_rows, D_MODEL), F32),
                     jax.ShapeDtypeStruct((n_rows - n_prompt_rows, D_MODEL), F32))
    else:
        out_shape = (jax.ShapeDtypeStruct((n_rows, D_MODEL), F32),)
    out = pl.pallas_call(
        functools.partial(_ffn_kernel, layer=layer, n_prompt_tiles=n_p, split_in=split_in, split_out=split_out,
                          with_pe=with_pe, final=final),
        grid=(WEIGHT_STEPS + n_p + n_s,),
        in_specs=((pair_specs if split_in else one_spec) + (pe_specs if with_pe else [])
                  + [_chunk_spec(w, layer) for w in big_w] + [_resident(w) for w in wts]),
        out_specs=tuple(pair_specs if split_out else one_spec),
        out_shape=out_shape,
        scratch_shapes=[pltpu.VMEM(w.shape[1:], BF16) for w in big_w],
        compiler_params=pltpu.CompilerParams(
            dimension_semantics=("arbitrary",),
            vmem_limit_bytes=V7X_VMEM_LIMIT_BYTES),
        name="ffn_pe" if with_pe else "ffn",
    )(*tok, *pe_inputs, *big_w, *wts)
    return out if split_out else out[0]


def kernel(x_prompt, x_sample, p_prompt, p_sample, cache_conv, cache_pool, ffn1_norm, ffn1_w_gate, ffn1_w_up, ffn1_w_down, mix_norm, w_in, gate_b, conv_dw_w, conv_dw_b, conv_ln_g, conv_ln_b, conv_w_out, sgu_ln_g, sgu_ln_b, sgu_w_s, sgu_b_s, sgu_w_out, pool_w, pool_scale, pool_w_out, w_out, ffn2_norm, ffn2_w_gate, ffn2_w_up, ffn2_w_down, pe_norm, pe_w_gate, pe_w_proj, final_norm):
    depth = ffn1_norm.shape[0]
    bp, sp, _ = x_prompt.shape
    bs, ss, _ = x_sample.shape
    n_p, n_s = bp * sp, bs * ss
    tok = (x_prompt.reshape(n_p, D_MODEL), x_sample.reshape(n_s, D_MODEL))
    pp_all = p_prompt.reshape(depth, n_p, D_PLE)
    ps_all = p_sample.reshape(depth, n_s, D_PLE)
    conv_zero = jnp.zeros((1, bp, CONV_HIST, WA), F32)
    pool_zero = jnp.zeros((1, bp, POOL_HIST, WC), F32)
    ffn1_mats = (ffn1_w_gate, ffn1_w_up, ffn1_w_down)
    ffn1_vecs = (ffn1_norm,)
    mixer_mats = (w_in, conv_w_out, sgu_w_out, pool_w_out, w_out)
    mixer_vecs = (
        mix_norm, gate_b,
        conv_dw_w, conv_dw_b, conv_ln_g, conv_ln_b,
        sgu_ln_g, sgu_ln_b, sgu_w_s, jnp.swapaxes(sgu_b_s, 1, 2),
        pool_w, pool_scale,
    )
    ffn2_mats = (ffn2_w_gate, ffn2_w_up, ffn2_w_down, pe_w_gate, pe_w_proj)
    ffn2_vecs = (ffn2_norm, pe_norm, final_norm.reshape(1, -1))
    conv_p, conv_s, pool_p, pool_s, v_p, v_s = [], [], [], [], [], []
    for i in range(depth):
        last = i == depth - 1
        h_all = _ffn_call(tok, n_p, (), ffn1_mats, ffn1_vecs, layer=i, split_out=False, final=False)
        h_all, (cp, pp, vp), (cs, ps, vs) = _mixer_call(h_all, n_p, sp, ss, (conv_zero, pool_zero),
                                                        (cache_conv, cache_pool), mixer_mats, mixer_vecs, layer=i)
        out = _ffn_call((h_all,), n_p, (pp_all, ps_all), ffn2_mats, ffn2_vecs, layer=i, split_out=last,
                        final=last)
        tok = out if last else (out,)

        conv_p.append(cp); conv_s.append(cs); pool_p.append(pp); pool_s.append(ps)
        v_p.append(vp); v_s.append(vs)
    y_prompt, y_sample = tok
    return (y_prompt.reshape(bp, sp, D_MODEL), y_sample.reshape(bs, ss, D_MODEL), jnp.stack(conv_p),
            jnp.stack(conv_s), jnp.stack(pool_p), jnp.stack(pool_s), jnp.stack(v_p), jnp.stack(v_s))
```
